```python
import jax, jax.numpy as jnp
from jax import lax
import numpy as np

D_MODEL = 2048
BATCH = 8
SEQ = 4096
DEPTH = 4

N_MIXERS = 2
N_CONV_LAYERS = (DEPTH + N_MIXERS - 1) // N_MIXERS
N_ATTN_LAYERS = DEPTH // N_MIXERS
CONV_WIDTH = 31
HEAD_DIM = 128
N_HEADS = D_MODEL // HEAD_DIM
N_KV_HEADS = 4
GROUP = N_HEADS // N_KV_HEADS
QKV_DIM = (N_HEADS + 2 * N_KV_HEADS) * HEAD_DIM
ROPE_THETA = 10000.0
ROPE_AXIS_DIM = HEAD_DIM // 2
GRID_W = 64
Q_BLOCK = 128
D_FF = 4 * D_MODEL
ALPHA = (2 * DEPTH) ** 0.25
BETA = (8 * DEPTH) ** -0.25
LN_EPS = 1e-5
RMS_EPS = 1e-6

kernel_name = "hybrid_conv_gqa_axialrope_deepnorm_adaln"


def layer_norm(x, g, b):
    xf = x.astype(jnp.float32)
    mu = jnp.mean(xf, axis=-1, keepdims=True)
    xc = xf - mu
    var = jnp.mean(xc * xc, axis=-1, keepdims=True)
    y = xc * lax.rsqrt(var + LN_EPS) * g.astype(jnp.float32) + b.astype(jnp.float32)
    return y.astype(x.dtype)


def rms_norm(x, g):
    xf = x.astype(jnp.float32)
    y = xf * lax.rsqrt(jnp.mean(xf * xf, axis=-1, keepdims=True) + RMS_EPS) * g.astype(jnp.float32)
    return y.astype(x.dtype)


def modulation(c_act, w, b):
    m = c_act @ w + b
    shift, scale, gate = jnp.split(m, 3, axis=-1)
    return shift[:, None, :], scale[:, None, :], gate[:, None, :]


def axial_rope_tables(seq_len, dtype):
    rows = seq_len // GRID_W
    row = jnp.repeat(jnp.arange(rows, dtype=jnp.int32), GRID_W)
    col = jnp.tile(jnp.arange(GRID_W, dtype=jnp.int32), rows)
    half = ROPE_AXIS_DIM // 2
    inv_freq = ROPE_THETA ** (-jnp.arange(half, dtype=jnp.float32) / half)
    ang_r = row.astype(jnp.float32)[:, None] * inv_freq[None, :]
    ang_c = col.astype(jnp.float32)[:, None] * inv_freq[None, :]
    return (jnp.cos(ang_r).astype(dtype), jnp.sin(ang_r).astype(dtype),
            jnp.cos(ang_c).astype(dtype), jnp.sin(ang_c).astype(dtype))


def rope_half(x, cos, sin):
    x1, x2 = jnp.split(x, 2, axis=-1)
    cos = cos[None, :, None, :]
    sin = sin[None, :, None, :]
    return jnp.concatenate([x1 * cos - x2 * sin, x2 * cos + x1 * sin], axis=-1)


def apply_axial_rope(x, tables):
    cos_r, sin_r, cos_c, sin_c = tables
    xr = x[..., :ROPE_AXIS_DIM]
    xc = x[..., ROPE_AXIS_DIM:]
    return jnp.concatenate([rope_half(xr, cos_r, sin_r), rope_half(xc, cos_c, sin_c)], axis=-1)


def conformer_conv(h, w_in, b_in, dw, dw_b, ln_g, ln_b, w_out):
    d = h.shape[-1]
    u = h @ w_in + b_in
    a, g = jnp.split(u, 2, axis=-1)
    u = a * jax.nn.sigmoid(g)
    pad = CONV_WIDTH // 2
    u = lax.conv_general_dilated(
        u, dw[:, None, :].astype(u.dtype), window_strides=(1,), padding=[(pad, pad)],
        dimension_numbers=("NWC", "WIO", "NWC"), feature_group_count=d) + dw_b
    u = jax.nn.silu(layer_norm(u, ln_g, ln_b))
    return u @ w_out


def gqa_axial(h, w_qkv, q_norm, k_norm, w_out, tables):
    bsz, seq, _ = h.shape
    qkv = h @ w_qkv
    q, k, v = jnp.split(qkv, [N_HEADS * HEAD_DIM, (N_HEADS + N_KV_HEADS) * HEAD_DIM], axis=-1)
    q = q.reshape(bsz, seq, N_HEADS, HEAD_DIM)
    k = k.reshape(bsz, seq, N_KV_HEADS, HEAD_DIM)
    v = v.reshape(bsz, seq, N_KV_HEADS, HEAD_DIM)
    q = apply_axial_rope(rms_norm(q, q_norm), tables)
    k = apply_axial_rope(rms_norm(k, k_norm), tables)
    n_blocks = seq // Q_BLOCK
    qb = q.reshape(bsz, n_blocks, Q_BLOCK, N_KV_HEADS, GROUP, HEAD_DIM).transpose(1, 0, 2, 3, 4, 5)
    scale = HEAD_DIM ** -0.5

    def attend(q_blk):
        s = jnp.einsum("bqkgd,bskd->bkgqs", q_blk, k).astype(jnp.float32) * scale
        p = jax.nn.softmax(s, axis=-1).astype(v.dtype)
        return jnp.einsum("bkgqs,bskd->bqkgd", p, v)

    o = lax.map(attend, qb)
    o = o.transpose(1, 0, 2, 3, 4, 5).reshape(bsz, seq, N_HEADS * HEAD_DIM)
    return o @ w_out


def sq_relu_mlp(h, w_in, w_out):
    u = jax.nn.relu(h @ w_in)
    return (u * u) @ w_out


def setup_inputs(seed: int = 0) -> dict:
    key = jax.random.key(seed)
    ks = jax.random.split(key, 22)
    f32 = jnp.float32
    nrm = lambda k, shape, s: jax.random.normal(k, shape, f32) * s
    D = D_MODEL
    return {
        "x": nrm(ks[0], (BATCH, SEQ, D), 1.0),
        "c": nrm(ks[1], (BATCH, D), 1.0),
        "mod_w": nrm(ks[2], (DEPTH, 2, D, 3 * D), 0.1 * D ** -0.5),
        "mod_b": nrm(ks[3], (DEPTH, 2, 3 * D), 0.01),
        "ln_g": 1.0 + nrm(ks[4], (DEPTH, 2, D), 0.02),
        "ln_b": nrm(ks[5], (DEPTH, 2, D), 0.02),
        "conv_w_in": nrm(ks[6], (N_CONV_LAYERS, D, 2 * D), D ** -0.5),
        "conv_b_in": nrm(ks[7], (N_CONV_LAYERS, 2 * D), 0.02),
        "conv_dw": nrm(ks[8], (N_CONV_LAYERS, CONV_WIDTH, D), CONV_WIDTH ** -0.5),
        "conv_dw_b": nrm(ks[9], (N_CONV_LAYERS, D), 0.02),
        "conv_ln_g": 1.0 + nrm(ks[10], (N_CONV_LAYERS, D), 0.02),
        "conv_ln_b": nrm(ks[11], (N_CONV_LAYERS, D), 0.02),
        "conv_w_out": nrm(ks[12], (N_CONV_LAYERS, D, D), BETA * D ** -0.5),
        "attn_w_qkv": nrm(ks[13], (N_ATTN_LAYERS, D, QKV_DIM), D ** -0.5),
        "attn_q_norm": 1.0 + nrm(ks[14], (N_ATTN_LAYERS, HEAD_DIM), 0.02),
        "attn_k_norm": 1.0 + nrm(ks[15], (N_ATTN_LAYERS, HEAD_DIM), 0.02),
        "attn_w_out": nrm(ks[16], (N_ATTN_LAYERS, N_HEADS * HEAD_DIM, D), BETA * (N_HEADS * HEAD_DIM) ** -0.5),
        "mlp_w_in": nrm(ks[17], (DEPTH, D, D_FF), D ** -0.5),
        "mlp_w_out": nrm(ks[18], (DEPTH, D_FF, D), BETA * D_FF ** -0.5),
    }


def reference(x, c, mod_w, mod_b, ln_g, ln_b, conv_w_in, conv_b_in, conv_dw, conv_dw_b,
              conv_ln_g, conv_ln_b, conv_w_out, attn_w_qkv, attn_q_norm, attn_k_norm,
              attn_w_out, mlp_w_in, mlp_w_out):
    c_act = jax.nn.silu(c)
    tables = axial_rope_tables(x.shape[1], x.dtype)
    for i in range(DEPTH):
        shift, scale, gate = modulation(c_act, mod_w[i, 0], mod_b[i, 0])
        h = x * (1 + scale) + shift
        j = i // N_MIXERS
        if i % N_MIXERS == 0:
            y = conformer_conv(h, conv_w_in[j], conv_b_in[j], conv_dw[j], conv_dw_b[j],
                               conv_ln_g[j], conv_ln_b[j], conv_w_out[j])
        else:
            y = gqa_axial(h, attn_w_qkv[j], attn_q_norm[j], attn_k_norm[j], attn_w_out[j], tables)
        x = layer_norm(ALPHA * x + (1 + gate) * y, ln_g[i, 0], ln_b[i, 0])
        shift, scale, gate = modulation(c_act, mod_w[i, 1], mod_b[i, 1])
        h = x * (1 + scale) + shift
        y = sq_relu_mlp(h, mlp_w_in[i], mlp_w_out[i])
        x = layer_norm(ALPHA * x + (1 + gate) * y, ln_g[i, 1], ln_b[i, 1])
    return x
```

```python
import functools

import jax
import jax.numpy as jnp
from jax import lax
from jax.experimental import pallas as pl
from jax.experimental.pallas import tpu as pltpu

F32 = jnp.float32
BF16 = jnp.bfloat16

GRID_W = 64
ROPE_THETA = 10000.0
LN_EPS = 1e-5
RMS_EPS = 1e-6

V7X_VMEM_BYTES = 64 * 1024 * 1024
VMEM_LIMIT_BYTES = V7X_VMEM_BYTES - 8 * 1024 * 1024
LANES = 128
CONV_HALO_ROWS = 16


def _params(*semantics):
    return pltpu.CompilerParams(dimension_semantics=semantics, vmem_limit_bytes=VMEM_LIMIT_BYTES)


def _layer_norm(z, g, b):
    mu = jnp.mean(z, axis=-1, keepdims=True)
    zc = z - mu
    var = jnp.mean(zc * zc, axis=-1, keepdims=True)
    return zc * lax.rsqrt(var + LN_EPS) * g + b


def _modulate(x, m_ref):
    return x * (1.0 + m_ref[0, 1:2, :]) + m_ref[0, 0:1, :]


def _residual_norm(x, y, m_ref, g_ref, b_ref, alpha):
    z = alpha * x + (1.0 + m_ref[0, 2:3, :]) * y
    return _layer_norm(z, g_ref[...], b_ref[...])


def _mod_kernel(c_ref, w_ref, b_ref, o_ref):
    c = c_ref[...]
    c_act = (c * jax.nn.sigmoid(c)).astype(BF16)
    w = w_ref[0].astype(BF16)
    o_ref[0] = jnp.dot(c_act, w, preferred_element_type=F32) + b_ref[0]


def _modulation_all(c, mod_w, mod_b, tn=768):
    depth, two, d, d3 = mod_w.shape
    n_sub = depth * two
    bsz = c.shape[0]
    w = mod_w.reshape(n_sub, d, d3)
    b = mod_b.reshape(n_sub, 1, d3)
    out = pl.pallas_call(
        _mod_kernel,
        grid=(n_sub, d3 // tn),
        in_specs=[
            pl.BlockSpec((bsz, d), lambda l, n: (0, 0)),
            pl.BlockSpec((1, d, tn), lambda l, n: (l, 0, n)),
            pl.BlockSpec((1, 1, tn), lambda l, n: (l, 0, n)),
        ],
        out_specs=pl.BlockSpec((1, bsz, tn), lambda l, n: (l, 0, n)),
        out_shape=jax.ShapeDtypeStruct((n_sub, bsz, d3), F32),
        compiler_params=_params("arbitrary", "arbitrary"),
        name="modulation",
    )(c, w, b)
    return out.reshape(n_sub, bsz, 3, d)


def _mlp_kernel(x_ref, m_ref, win_ref, wout_ref, g_ref, b_ref, o_ref, h_ref, *, alpha):
    f = pl.program_id(2)

    @pl.when(f == 0)
    def _():
        h_ref[...] = _modulate(x_ref[0], m_ref).astype(BF16)
        o_ref[0] = jnp.zeros(o_ref.shape[1:], F32)

    u = jnp.dot(h_ref[...], win_ref[...], preferred_element_type=F32)
    u = jnp.maximum(u, 0.0)
    u = (u * u).astype(BF16)
    o_ref[0] += jnp.dot(u, wout_ref[...], preferred_element_type=F32)

    @pl.when(f == pl.num_programs(2) - 1)
    def _():
        o_ref[0] = _residual_norm(x_ref[0], o_ref[0], m_ref, g_ref, b_ref, alpha)


def _mlp_layer(x, m, w_in, w_out, g, b, alpha, tm=1024, tf=512):
    bsz, seq, d = x.shape
    d_ff = w_in.shape[1]
    return pl.pallas_call(
        functools.partial(_mlp_kernel, alpha=alpha),
        grid=(bsz, seq // tm, d_ff // tf),
        in_specs=[
            pl.BlockSpec((1, tm, d), lambda bi, i, f: (bi, i, 0)),
            pl.BlockSpec((1, 3, d), lambda bi, i, f: (bi, 0, 0)),
            pl.BlockSpec((d, tf), lambda bi, i, f: (0, f)),
            pl.BlockSpec((tf, d), lambda bi, i, f: (f, 0)),
            pl.BlockSpec((1, d), lambda bi, i, f: (0, 0)),
            pl.BlockSpec((1, d), lambda bi, i, f: (0, 0)),
        ],
        out_specs=pl.BlockSpec((1, tm, d), lambda bi, i, f: (bi, i, 0)),
        out_shape=jax.ShapeDtypeStruct(x.shape, F32),
        scratch_shapes=[pltpu.VMEM((tm, d), BF16)],
        compiler_params=_params("parallel", "parallel", "arbitrary"),
        name="mlp",
    )(x, m, w_in, w_out, g, b)


def _proj_norm_kernel(a_ref, x_ref, m_ref, w_ref, g_ref, b_ref, o_ref, *, alpha):
    y = jnp.dot(a_ref[0], w_ref[...], preferred_element_type=F32)
    o_ref[0] = _residual_norm(x_ref[0], y, m_ref, g_ref, b_ref, alpha)


def _proj_norm(a, x, m, w, g, b, alpha, tm=512):
    bsz, seq, d = x.shape
    k = a.shape[-1]
    return pl.pallas_call(
        functools.partial(_proj_norm_kernel, alpha=alpha),
        grid=(bsz, seq // tm),
        in_specs=[
            pl.BlockSpec((1, tm, k), lambda bi, i: (bi, i, 0)),
            pl.BlockSpec((1, tm, d), lambda bi, i: (bi, i, 0)),
            pl.BlockSpec((1, 3, d), lambda bi, i: (bi, 0, 0)),
            pl.BlockSpec((k, d), lambda bi, i: (0, 0)),
            pl.BlockSpec((1, d), lambda bi, i: (0, 0)),
            pl.BlockSpec((1, d), lambda bi, i: (0, 0)),
        ],
        out_specs=pl.BlockSpec((1, tm, d), lambda bi, i: (bi, i, 0)),
        out_shape=jax.ShapeDtypeStruct(x.shape, F32),
        compiler_params=_params("parallel", "parallel"),
        name="proj_norm",
    )(a, x, m, w, g, b)


def _qkv_kernel(x_ref, m_ref, w_ref, qg_ref, kg_ref, cos_ref, sin_ref, q_ref, k_ref, v_ref,
                *, n_heads, n_kv, hd):
    h = _modulate(x_ref[0], m_ref).astype(BF16)
    qkv = jnp.dot(h, w_ref[...], preferred_element_type=F32)
    cos = cos_ref[...]
    sin = sin_ref[...]
    lane = lax.broadcasted_iota(jnp.int32, cos.shape, 1)
    low = (lane % (hd // 2)) < (hd // 4)

    def norm_rope(xh, gain):
        ms = jnp.mean(xh * xh, axis=-1, keepdims=True)
        y = xh * lax.rsqrt(ms + RMS_EPS) * gain
        partner = jnp.where(low, pltpu.roll(y, hd - hd // 4, 1), pltpu.roll(y, hd // 4, 1))
        return y * cos + partner * sin

    for hi in range(n_heads):
        q_ref[0, :, hi * hd:(hi + 1) * hd] = norm_rope(
            qkv[:, hi * hd:(hi + 1) * hd], qg_ref[...]).astype(BF16)
    k0 = n_heads * hd
    for hi in range(n_kv):
        k_ref[0, :, hi * hd:(hi + 1) * hd] = norm_rope(
            qkv[:, k0 + hi * hd:k0 + (hi + 1) * hd], kg_ref[...]).astype(BF16)
    v0 = k0 + n_kv * hd
    v_ref[0] = qkv[:, v0:].astype(BF16)


def _qkv_proj(x, m, w_qkv, q_gain, k_gain, cos, sin, n_heads, n_kv, hd, tm=512):
    bsz, seq, d = x.shape
    qkv_dim = w_qkv.shape[1]
    return pl.pallas_call(
        functools.partial(_qkv_kernel, n_heads=n_heads, n_kv=n_kv, hd=hd),
        grid=(bsz, seq // tm),
        in_specs=[
            pl.BlockSpec((1, tm, d), lambda bi, i: (bi, i, 0)),
            pl.BlockSpec((1, 3, d), lambda bi, i: (bi, 0, 0)),
            pl.BlockSpec((d, qkv_dim), lambda bi, i: (0, 0)),
            pl.BlockSpec((1, hd), lambda bi, i: (0, 0)),
            pl.BlockSpec((1, hd), lambda bi, i: (0, 0)),
            pl.BlockSpec((tm, hd), lambda bi, i: (i, 0)),
            pl.BlockSpec((tm, hd), lambda bi, i: (i, 0)),
        ],
        out_specs=[
            pl.BlockSpec((1, tm, n_heads * hd), lambda bi, i: (bi, i, 0)),
            pl.BlockSpec((1, tm, n_kv * hd), lambda bi, i: (bi, i, 0)),
            pl.BlockSpec((1, tm, n_kv * hd), lambda bi, i: (bi, i, 0)),
        ],
        out_shape=[
            jax.ShapeDtypeStruct((bsz, seq, n_heads * hd), BF16),
            jax.ShapeDtypeStruct((bsz, seq, n_kv * hd), BF16),
            jax.ShapeDtypeStruct((bsz, seq, n_kv * hd), BF16),
        ],
        compiler_params=_params("parallel", "parallel"),
        name="qkv_proj",
    )(x, m, w_qkv, q_gain, k_gain, cos, sin)


def _attn_kernel(q_ref, k_ref, v_ref, o_ref, *, group, hd):
    k = k_ref[0]
    v = v_ref[0]
    for gi in range(group):
        q = q_ref[0, :, gi * hd:(gi + 1) * hd]
        s = lax.dot_general(q, k, (((1,), (1,)), ((), ())), preferred_element_type=F32)
        p = jnp.exp(s - jnp.max(s, axis=-1, keepdims=True))
        denom = jnp.sum(p, axis=-1, keepdims=True)
        o = jnp.dot(p.astype(BF16), v, preferred_element_type=F32)
        o_ref[0, :, gi * hd:(gi + 1) * hd] = (o / denom).astype(BF16)


def _attention(q, k, v, n_kv, hd, tq=256):
    bsz, seq, qd = q.shape
    group = qd // (n_kv * hd)
    gw = group * hd
    return pl.pallas_call(
        functools.partial(_attn_kernel, group=group, hd=hd),
        grid=(bsz, n_kv, seq // tq),
        in_specs=[
            pl.BlockSpec((1, tq, gw), lambda bi, kv, i: (bi, i, kv)),
            pl.BlockSpec((1, seq, hd), lambda bi, kv, i: (bi, 0, kv)),
            pl.BlockSpec((1, seq, hd), lambda bi, kv, i: (bi, 0, kv)),
        ],
        out_specs=pl.BlockSpec((1, tq, gw), lambda bi, kv, i: (bi, i, kv)),
        out_shape=jax.ShapeDtypeStruct(q.shape, BF16),
        compiler_params=_params("parallel", "parallel", "parallel"),
        name="attention",
    )(q, k, v)


def _glu_kernel(x_ref, m_ref, w_ref, b_ref, o_ref):
    d = o_ref.shape[-1]
    h = _modulate(x_ref[0], m_ref).astype(BF16)
    u = jnp.dot(h, w_ref[...], preferred_element_type=F32) + b_ref[...]
    o_ref[0] = u[:, :d] * jax.nn.sigmoid(u[:, d:])


def _glu_proj(x, m, w_in, b_in, tm=512):
    bsz, seq, d = x.shape
    return pl.pallas_call(
        _glu_kernel,
        grid=(bsz, seq // tm),
        in_specs=[
            pl.BlockSpec((1, tm, d), lambda bi, i: (bi, i, 0)),
            pl.BlockSpec((1, 3, d), lambda bi, i: (bi, 0, 0)),
            pl.BlockSpec((d, 2 * d), lambda bi, i: (0, 0)),
            pl.BlockSpec((1, 2 * d), lambda bi, i: (0, 0)),
        ],
        out_specs=pl.BlockSpec((1, tm, d), lambda bi, i: (bi, i, 0)),
        out_shape=jax.ShapeDtypeStruct(x.shape, F32),
        compiler_params=_params("parallel", "parallel"),
        name="glu_proj",
    )(x, m, w_in, b_in)


def _conv_out_kernel(u_ref, up_ref, un_ref, x_ref, m_ref, dw_ref, dwb_ref, cg_ref, cb_ref, w_ref,
                     g_ref, b_ref, o_ref, buf_ref, c_ref, *, alpha, width, rows, cols):
    i = pl.program_id(1)
    ts, d = u_ref.shape[1:]
    halo = CONV_HALO_ROWS
    pad = width // 2
    buf_ref[0:halo, :] = jnp.where(i > 0, up_ref[0], 0.0)
    buf_ref[halo:halo + ts, :] = u_ref[0]
    buf_ref[halo + ts:, :] = jnp.where(i < pl.num_programs(1) - 1, un_ref[0], 0.0)

    def row_chunk(r, carry):
        r0 = pl.multiple_of(r * rows, rows)
        for c0 in range(0, d, cols):
            win = buf_ref[pl.ds(r0, rows + 2 * halo), c0:c0 + cols]
            acc = jnp.zeros((rows, cols), F32)
            for r in range(8):
                taps = [t for t in range(width) if (halo - pad + t) % 8 == r]
                shifted = win[r:r + rows + 2 * halo - 8]
                for t in taps:
                    a8 = halo - pad + t - r
                    acc = acc + shifted[a8:a8 + rows] * dw_ref[t:t + 1, c0:c0 + cols]
            c_ref[pl.ds(r0, rows), c0:c0 + cols] = acc
        return carry

    lax.fori_loop(0, ts // rows, row_chunk, 0)
    v = _layer_norm(c_ref[...] + dwb_ref[...], cg_ref[...], cb_ref[...])
    v = (v * jax.nn.sigmoid(v)).astype(BF16)
    y = jnp.dot(v, w_ref[...], preferred_element_type=F32)
    o_ref[0] = _residual_norm(x_ref[0], y, m_ref, g_ref, b_ref, alpha)


def _conv_out(u, x, m, dw, dw_b, cg, cb, w_out, g, b, alpha, ts=512, rows=64, cols=256):
    bsz, seq, d = x.shape
    width = dw.shape[0]
    halo = CONV_HALO_ROWS
    assert width // 2 <= halo and ts % halo == 0
    per = ts // halo
    n_halo = seq // halo
    return pl.pallas_call(
        functools.partial(_conv_out_kernel, alpha=alpha, width=width, rows=rows, cols=cols),
        grid=(bsz, seq // ts),
        in_specs=[
            pl.BlockSpec((1, ts, d), lambda bi, i: (bi, i, 0)),
            pl.BlockSpec((1, halo, d), lambda bi, i: (bi, jnp.maximum(i * per - 1, 0), 0)),
            pl.BlockSpec((1, halo, d), lambda bi, i: (bi, jnp.minimum((i + 1) * per, n_halo - 1), 0)),
            pl.BlockSpec((1, ts, d), lambda bi, i: (bi, i, 0)),
            pl.BlockSpec((1, 3, d), lambda bi, i: (bi, 0, 0)),
            pl.BlockSpec((width, d), lambda bi, i: (0, 0)),
            pl.BlockSpec((1, d), lambda bi, i: (0, 0)),
            pl.BlockSpec((1, d), lambda bi, i: (0, 0)),
            pl.BlockSpec((1, d), lambda bi, i: (0, 0)),
            pl.BlockSpec((d, d), lambda bi, i: (0, 0)),
            pl.BlockSpec((1, d), lambda bi, i: (0, 0)),
            pl.BlockSpec((1, d), lambda bi, i: (0, 0)),
        ],
        out_specs=pl.BlockSpec((1, ts, d), lambda bi, i: (bi, i, 0)),
        out_shape=jax.ShapeDtypeStruct(x.shape, F32),
        scratch_shapes=[pltpu.VMEM((ts + 2 * halo, d), F32), pltpu.VMEM((ts, d), F32)],
        compiler_params=_params("parallel", "parallel"),
        name="conv_out",
    )(u, u, u, x, m, dw, dw_b, cg, cb, w_out, g, b)


def _rope_tables(seq, hd):
    axis_dim = hd // 2
    half = axis_dim // 2
    t = jnp.arange(seq, dtype=jnp.int32)
    inv_freq = ROPE_THETA ** (-jnp.arange(half, dtype=F32) / half)
    ang_r = (t // GRID_W).astype(F32)[:, None] * inv_freq[None, :]
    ang_c = (t % GRID_W).astype(F32)[:, None] * inv_freq[None, :]
    cos_r, sin_r, cos_c, sin_c = jnp.cos(ang_r), jnp.sin(ang_r), jnp.cos(ang_c), jnp.sin(ang_c)
    cos = jnp.concatenate([cos_r, cos_r, cos_c, cos_c], axis=-1)
    sin = jnp.concatenate([-sin_r, sin_r, -sin_c, sin_c], axis=-1)
    return cos, sin


def kernel(x, c, mod_w, mod_b, ln_g, ln_b, conv_w_in, conv_b_in, conv_dw, conv_dw_b, conv_ln_g,
           conv_ln_b, conv_w_out, attn_w_qkv, attn_q_norm, attn_k_norm, attn_w_out, mlp_w_in,
           mlp_w_out):
    depth = mod_w.shape[0]
    n_mixers = 2
    bsz, seq, d = x.shape
    hd = attn_q_norm.shape[-1]
    n_heads = attn_w_out.shape[1] // hd
    n_kv = (attn_w_qkv.shape[-1] - n_heads * hd) // (2 * hd)
    alpha = (2 * depth) ** 0.25
    sm_scale = hd ** -0.5

    mods = _modulation_all(c, mod_w, mod_b)
    cos, sin = _rope_tables(seq, hd)
    row = lambda a: a.reshape(1, -1)

    for i in range(depth):
        j = i // n_mixers
        m = mods[2 * i]
        g, b = row(ln_g[i, 0]), row(ln_b[i, 0])
        if i % n_mixers == 0:
            u = _glu_proj(x, m, conv_w_in[j].astype(BF16), row(conv_b_in[j]))
            x = _conv_out(u, x, m, conv_dw[j], row(conv_dw_b[j]), row(conv_ln_g[j]),
                          row(conv_ln_b[j]), conv_w_out[j].astype(BF16), g, b, alpha)
        else:
            q, k, v = _qkv_proj(x, m, attn_w_qkv[j].astype(BF16), row(attn_q_norm[j] * sm_scale),
                                row(attn_k_norm[j]), cos, sin, n_heads, n_kv, hd)
            o = _attention(q, k, v, n_kv, hd)
            x = _proj_norm(o, x, m, attn_w_out[j].astype(BF16), g, b, alpha)
        m = mods[2 * i + 1]
        x = _mlp_layer(x, m, mlp_w_in[i].astype(BF16), mlp_w_out[i].astype(BF16),
                       row(ln_g[i, 1]), row(ln_b[i, 1]), alpha)
    return x
```

```python
import functools

import jax
import jax.numpy as jnp
from jax import lax
from jax.experimental import pallas as pl
from jax.experimental.pallas import tpu as pltpu

F32 = jnp.float32
BF16 = jnp.bfloat16

GRID_W = 64
ROPE_THETA = 10000.0
LN_EPS = 1e-5
RMS_EPS = 1e-6
LOG2_E = 1.4426950408889634

V7X_VMEM_BYTES = 64 * 1024 * 1024
VMEM_LIMIT_BYTES = V7X_VMEM_BYTES - 8 * 1024 * 1024
LANES = 128
CONV_HALO_ROWS = 16


def _params(*semantics):
    return pltpu.CompilerParams(dimension_semantics=semantics, vmem_limit_bytes=VMEM_LIMIT_BYTES)


def _layer_norm(z, g, b):
    mu = jnp.mean(z, axis=-1, keepdims=True)
    zc = z - mu
    var = jnp.mean(zc * zc, axis=-1, keepdims=True)
    return zc * lax.rsqrt(var + LN_EPS) * g + b


def _modulate(x, m_ref):
    return x * (1.0 + m_ref[0, 1:2, :]) + m_ref[0, 0:1, :]


def _residual_norm(x, y, m_ref, g_ref, b_ref, alpha):
    z = alpha * x + (1.0 + m_ref[0, 2:3, :]) * y
    return _layer_norm(z, g_ref[...], b_ref[...])


def _mod_kernel(c_ref, w_ref, b_ref, o_ref):
    c = c_ref[...]
    c_act = (c * jax.nn.sigmoid(c)).astype(BF16)
    w = w_ref[0].astype(BF16)
    o_ref[0] = jnp.dot(c_act, w, preferred_element_type=F32) + b_ref[0]


def _modulation_all(c, mod_w, mod_b, tn=768):
    depth, two, d, d3 = mod_w.shape
    n_sub = depth * two
    bsz = c.shape[0]
    w = mod_w.reshape(n_sub, d, d3)
    b = mod_b.reshape(n_sub, 1, d3)
    out = pl.pallas_call(
        _mod_kernel,
        grid=(n_sub, d3 // tn),
        in_specs=[
            pl.BlockSpec((bsz, d), lambda l, n: (0, 0)),
            pl.BlockSpec((1, d, tn), lambda l, n: (l, 0, n)),
            pl.BlockSpec((1, 1, tn), lambda l, n: (l, 0, n)),
        ],
        out_specs=pl.BlockSpec((1, bsz, tn), lambda l, n: (l, 0, n)),
        out_shape=jax.ShapeDtypeStruct((n_sub, bsz, d3), F32),
        compiler_params=_params("arbitrary", "arbitrary"),
        name="modulation",
    )(c, w, b)
    return out.reshape(n_sub, bsz, 3, d)


def _mlp_kernel(x_ref, m_ref, win_ref, wout_ref, g_ref, b_ref, o_ref, h_ref, *, alpha):
    f = pl.program_id(2)

    @pl.when(f == 0)
    def _():
        h_ref[...] = _modulate(x_ref[0], m_ref).astype(BF16)
        o_ref[0] = jnp.zeros(o_ref.shape[1:], F32)

    u = jnp.dot(h_ref[...], win_ref[...], preferred_element_type=F32)
    u = jnp.maximum(u, 0.0)
    u = (u * u).astype(BF16)
    o_ref[0] += jnp.dot(u, wout_ref[...], preferred_element_type=F32)

    @pl.when(f == pl.num_programs(2) - 1)
    def _():
        o_ref[0] = _residual_norm(x_ref[0], o_ref[0], m_ref, g_ref, b_ref, alpha)


def _mlp_layer(x, m, w_in, w_out, g, b, alpha, tm=1024, tf=512):
    bsz, seq, d = x.shape
    d_ff = w_in.shape[1]
    return pl.pallas_call(
        functools.partial(_mlp_kernel, alpha=alpha),
        grid=(bsz, seq // tm, d_ff // tf),
        in_specs=[
            pl.BlockSpec((1, tm, d), lambda bi, i, f: (bi, i, 0)),
            pl.BlockSpec((1, 3, d), lambda bi, i, f: (bi, 0, 0)),
            pl.BlockSpec((d, tf), lambda bi, i, f: (0, f)),
            pl.BlockSpec((tf, d), lambda bi, i, f: (f, 0)),
            pl.BlockSpec((1, d), lambda bi, i, f: (0, 0)),
            pl.BlockSpec((1, d), lambda bi, i, f: (0, 0)),
        ],
        out_specs=pl.BlockSpec((1, tm, d), lambda bi, i, f: (bi, i, 0)),
        out_shape=jax.ShapeDtypeStruct(x.shape, F32),
        scratch_shapes=[pltpu.VMEM((tm, d), BF16)],
        compiler_params=_params("parallel", "parallel", "arbitrary"),
        name="mlp",
    )(x, m, w_in, w_out, g, b)


def _proj_norm_kernel(a_ref, x_ref, m_ref, w_ref, g_ref, b_ref, o_ref, *, alpha):
    y = jnp.dot(a_ref[0], w_ref[...], preferred_element_type=F32)
    o_ref[0] = _residual_norm(x_ref[0], y, m_ref, g_ref, b_ref, alpha)


def _proj_norm(a, x, m, w, g, b, alpha, tm=512):
    bsz, seq, d = x.shape
    k = a.shape[-1]
    return pl.pallas_call(
        functools.partial(_proj_norm_kernel, alpha=alpha),
        grid=(bsz, seq // tm),
        in_specs=[
            pl.BlockSpec((1, tm, k), lambda bi, i: (bi, i, 0)),
            pl.BlockSpec((1, tm, d), lambda bi, i: (bi, i, 0)),
            pl.BlockSpec((1, 3, d), lambda bi, i: (bi, 0, 0)),
            pl.BlockSpec((k, d), lambda bi, i: (0, 0)),
            pl.BlockSpec((1, d), lambda bi, i: (0, 0)),
            pl.BlockSpec((1, d), lambda bi, i: (0, 0)),
        ],
        out_specs=pl.BlockSpec((1, tm, d), lambda bi, i: (bi, i, 0)),
        out_shape=jax.ShapeDtypeStruct(x.shape, F32),
        compiler_params=_params("parallel", "parallel"),
        name="proj_norm",
    )(a, x, m, w, g, b)


def _qkv_kernel(x_ref, m_ref, w_ref, qg_ref, kg_ref, cos_ref, sin_ref, q_ref, k_ref, v_ref,
                *, n_heads, n_kv, hd):
    h = _modulate(x_ref[0], m_ref).astype(BF16)
    qkv = jnp.dot(h, w_ref[...], preferred_element_type=F32)
    cos = cos_ref[...]
    sin = sin_ref[...]
    lane = lax.broadcasted_iota(jnp.int32, cos.shape, 1)
    low = (lane % (hd // 2)) < (hd // 4)

    def norm_rope(xh, gain):
        ms = jnp.mean(xh * xh, axis=-1, keepdims=True)
        y = xh * lax.rsqrt(ms + RMS_EPS) * gain
        partner = jnp.where(low, pltpu.roll(y, hd - hd // 4, 1), pltpu.roll(y, hd // 4, 1))
        return y * cos + partner * sin

    for hi in range(n_heads):
        q_ref[0, :, hi * hd:(hi + 1) * hd] = norm_rope(
            qkv[:, hi * hd:(hi + 1) * hd], qg_ref[...]).astype(BF16)
    k0 = n_heads * hd
    for hi in range(n_kv):
        k_ref[0, :, hi * hd:(hi + 1) * hd] = norm_rope(
            qkv[:, k0 + hi * hd:k0 + (hi + 1) * hd], kg_ref[...]).astype(BF16)
    v0 = k0 + n_kv * hd
    v_ref[0] = qkv[:, v0:].astype(BF16)


def _qkv_proj(x, m, w_qkv, q_gain, k_gain, cos, sin, n_heads, n_kv, hd, tm=512):
    bsz, seq, d = x.shape
    qkv_dim = w_qkv.shape[1]
    return pl.pallas_call(
        functools.partial(_qkv_kernel, n_heads=n_heads, n_kv=n_kv, hd=hd),
        grid=(bsz, seq // tm),
        in_specs=[
            pl.BlockSpec((1, tm, d), lambda bi, i: (bi, i, 0)),
            pl.BlockSpec((1, 3, d), lambda bi, i: (bi, 0, 0)),
            pl.BlockSpec((d, qkv_dim), lambda bi, i: (0, 0)),
            pl.BlockSpec((1, hd), lambda bi, i: (0, 0)),
            pl.BlockSpec((1, hd), lambda bi, i: (0, 0)),
            pl.BlockSpec((tm, hd), lambda bi, i: (i, 0)),
            pl.BlockSpec((tm, hd), lambda bi, i: (i, 0)),
        ],
        out_specs=[
            pl.BlockSpec((1, tm, n_heads * hd), lambda bi, i: (bi, i, 0)),
            pl.BlockSpec((1, tm, n_kv * hd), lambda bi, i: (bi, i, 0)),
            pl.BlockSpec((1, tm, n_kv * hd), lambda bi, i: (bi, i, 0)),
        ],
        out_shape=[
            jax.ShapeDtypeStruct((bsz, seq, n_heads * hd), BF16),
            jax.ShapeDtypeStruct((bsz, seq, n_kv * hd), BF16),
            jax.ShapeDtypeStruct((bsz, seq, n_kv * hd), BF16),
        ],
        compiler_params=_params("parallel", "parallel"),
        name="qkv_proj",
    )(x, m, w_qkv, q_gain, k_gain, cos, sin)


def _attn_kernel(q_ref, k_ref, v_ref, o_ref, s_ref, vaug_ref, *, group, hd, tq):
    seq = k_ref.shape[1]
    n_tiles = seq // tq
    lane = lax.broadcasted_iota(jnp.int32, (seq, hd), 1)
    vaug_ref[:, :hd] = v_ref[0]
    vaug_ref[:, hd:] = (lane == 0).astype(BF16)

    def logits(row0, gi):
        q = q_ref[0, pl.ds(row0, tq), gi * hd:(gi + 1) * hd]
        return lax.dot_general(q, k_ref[0], (((1,), (1,)), ((), ())), preferred_element_type=F32)

    s_ref[0] = logits(0, 0)

    def tile(it, carry):
        row0 = pl.multiple_of(it * tq, tq)
        row_next = pl.multiple_of(jnp.minimum(it + 1, n_tiles - 1) * tq, tq)
        for gi in range(group):
            cur = gi % 2
            if gi + 1 < group:
                s_ref[1 - cur] = logits(row0, gi + 1)
            else:
                s_ref[1 - cur] = logits(row_next, 0)
            s = s_ref[cur]
            p = jnp.exp2(s - jnp.max(s, axis=-1, keepdims=True)).astype(BF16)
            oa = jnp.dot(p, vaug_ref[...], preferred_element_type=F32)
            o_ref[0, pl.ds(row0, tq), gi * hd:(gi + 1) * hd] = (
                oa[:, :hd] / oa[:, hd:hd + 1]).astype(BF16)
        return carry

    lax.fori_loop(0, n_tiles, tile, 0)


def _attention(q, k, v, n_kv, hd, tq=256):
    bsz, seq, qd = q.shape
    group = qd // (n_kv * hd)
    assert group % 2 == 0, "logit buffers alternate per head and must realign every query tile"
    gw = group * hd
    return pl.pallas_call(
        functools.partial(_attn_kernel, group=group, hd=hd, tq=tq),
        grid=(bsz, n_kv),
        in_specs=[
            pl.BlockSpec((1, seq, gw), lambda bi, kv: (bi, 0, kv)),
            pl.BlockSpec((1, seq, hd), lambda bi, kv: (bi, 0, kv)),
            pl.BlockSpec((1, seq, hd), lambda bi, kv: (bi, 0, kv)),
        ],
        out_specs=pl.BlockSpec((1, seq, gw), lambda bi, kv: (bi, 0, kv)),
        out_shape=jax.ShapeDtypeStruct(q.shape, BF16),
        scratch_shapes=[pltpu.VMEM((2, tq, seq), F32), pltpu.VMEM((seq, 2 * hd), BF16)],
        compiler_params=_params("parallel", "parallel"),
        name="attention",
    )(q, k, v)


def _conv_layer_kernel(x_ref, xp_ref, xn_ref, m_ref, win_ref, bin_ref, dw_ref, dwb_ref, cg_ref,
                       cb_ref, wout_ref, g_ref, b_ref, o_ref, h_ref, buf_ref, c_ref,
                       *, alpha, width, rows, chunk):
    i = pl.program_id(1)
    ts, d = x_ref.shape[1:]
    halo = CONV_HALO_ROWS
    pad = width // 2
    first = i == 0
    last = i == pl.num_programs(1) - 1

    h_ref[0:halo, :] = _modulate(xp_ref[0], m_ref).astype(BF16)
    h_ref[halo:halo + ts, :] = _modulate(x_ref[0], m_ref).astype(BF16)
    h_ref[halo + ts:, :] = _modulate(xn_ref[0], m_ref).astype(BF16)

    for c0 in range(0, d, chunk):
        h = h_ref[...]
        a = jnp.dot(h, win_ref[:, c0:c0 + chunk], preferred_element_type=F32) + bin_ref[:, c0:c0 + chunk]
        gate = jnp.dot(h, win_ref[:, d + c0:d + c0 + chunk],
                       preferred_element_type=F32) + bin_ref[:, d + c0:d + c0 + chunk]
        glu = a * jax.nn.sigmoid(gate)
        for l0 in range(0, chunk, LANES):
            j = (c0 + l0) // LANES
            buf_ref[j, 0:halo, :] = jnp.where(first, 0.0, glu[0:halo, l0:l0 + LANES])
            buf_ref[j, halo:halo + ts, :] = glu[halo:halo + ts, l0:l0 + LANES]
            buf_ref[j, halo + ts:, :] = jnp.where(last, 0.0, glu[halo + ts:, l0:l0 + LANES])
        for l0 in range(0, chunk, LANES):
            j = (c0 + l0) // LANES
            for r0 in range(0, ts, rows):
                acc = jnp.zeros((rows, LANES), F32)
                for t in range(width):
                    tap = buf_ref[j, r0 + halo - pad + t:r0 + halo - pad + t + rows, :]
                    acc = acc + tap * dw_ref[j, t:t + 1, :]
                c_ref[r0:r0 + rows, c0 + l0:c0 + l0 + LANES] = acc

    v = _layer_norm(c_ref[...] + dwb_ref[...], cg_ref[...], cb_ref[...])
    v = (v * jax.nn.sigmoid(v)).astype(BF16)
    y = jnp.dot(v, wout_ref[...], preferred_element_type=F32)
    o_ref[0] = _residual_norm(x_ref[0], y, m_ref, g_ref, b_ref, alpha)


def _conv_layer(x, m, w_in, b_in, dw, dw_b, cg, cb, w_out, g, b, alpha, ts=512, rows=64, chunk=512):
    bsz, seq, d = x.shape
    width = dw.shape[0]
    halo = CONV_HALO_ROWS
    assert width // 2 <= halo and ts % halo == 0 and d % chunk == 0 and chunk % LANES == 0
    nl = d // LANES
    per = ts // halo
    n_halo = seq // halo
    dw3 = dw.reshape(width, nl, LANES).transpose(1, 0, 2)
    const = lambda bi, i: (0, 0)
    resident = pl.Buffered(1)
    return pl.pallas_call(
        functools.partial(_conv_layer_kernel, alpha=alpha, width=width, rows=rows, chunk=chunk),
        grid=(bsz, seq // ts),
        in_specs=[
            pl.BlockSpec((1, ts, d), lambda bi, i: (bi, i, 0)),
            pl.BlockSpec((1, halo, d), lambda bi, i: (bi, jnp.maximum(i * per - 1, 0), 0)),
            pl.BlockSpec((1, halo, d), lambda bi, i: (bi, jnp.minimum((i + 1) * per, n_halo - 1), 0)),
            pl.BlockSpec((1, 3, d), lambda bi, i: (bi, 0, 0)),
            pl.BlockSpec((d, 2 * d), const, pipeline_mode=resident),
            pl.BlockSpec((1, 2 * d), const),
            pl.BlockSpec((nl, width, LANES), lambda bi, i: (0, 0, 0)),
            pl.BlockSpec((1, d), const),
            pl.BlockSpec((1, d), const),
            pl.BlockSpec((1, d), const),
            pl.BlockSpec((d, d), const, pipeline_mode=resident),
            pl.BlockSpec((1, d), const),
            pl.BlockSpec((1, d), const),
        ],
        out_specs=pl.BlockSpec((1, ts, d), lambda bi, i: (bi, i, 0)),
        out_shape=jax.ShapeDtypeStruct(x.shape, F32),
        scratch_shapes=[
            pltpu.VMEM((ts + 2 * halo, d), BF16),
            pltpu.VMEM((nl, ts + 2 * halo, LANES), F32),
            pltpu.VMEM((ts, d), F32),
        ],
        compiler_params=_params("parallel", "parallel"),
        name="conv_layer",
    )(x, x, x, m, w_in, b_in, dw3, dw_b, cg, cb, w_out, g, b)


def _rope_tables(seq, hd):
    axis_dim = hd // 2
    half = axis_dim // 2
    t = jnp.arange(seq, dtype=jnp.int32)
    inv_freq = ROPE_THETA ** (-jnp.arange(half, dtype=F32) / half)
    ang_r = (t // GRID_W).astype(F32)[:, None] * inv_freq[None, :]
    ang_c = (t % GRID_W).astype(F32)[:, None] * inv_freq[None, :]
    cos_r, sin_r, cos_c, sin_c = jnp.cos(ang_r), jnp.sin(ang_r), jnp.cos(ang_c), jnp.sin(ang_c)
    cos = jnp.concatenate([cos_r, cos_r, cos_c, cos_c], axis=-1)
    sin = jnp.concatenate([-sin_r, sin_r, -sin_c, sin_c], axis=-1)
    return cos, sin


def kernel(x, c, mod_w, mod_b, ln_g, ln_b, conv_w_in, conv_b_in, conv_dw, conv_dw_b, conv_ln_g,
           conv_ln_b, conv_w_out, attn_w_qkv, attn_q_norm, attn_k_norm, attn_w_out, mlp_w_in,
           mlp_w_out):
    depth = mod_w.shape[0]
    n_mixers = 2
    bsz, seq, d = x.shape
    hd = attn_q_norm.shape[-1]
    n_heads = attn_w_out.shape[1] // hd
    n_kv = (attn_w_qkv.shape[-1] - n_heads * hd) // (2 * hd)
    alpha = (2 * depth) ** 0.25
    sm_scale = hd ** -0.5

    mods = _modulation_all(c, mod_w, mod_b)
    cos, sin = _rope_tables(seq, hd)
    row = lambda a: a.reshape(1, -1)

    for i in range(depth):
        j = i // n_mixers
        m = mods[2 * i]
        g, b = row(ln_g[i, 0]), row(ln_b[i, 0])
        if i % n_mixers == 0:
            x = _conv_layer(x, m, conv_w_in[j].astype(BF16), row(conv_b_in[j]), conv_dw[j],
                            row(conv_dw_b[j]), row(conv_ln_g[j]), row(conv_ln_b[j]),
                            conv_w_out[j].astype(BF16), g, b, alpha)
        else:
            q, k, v = _qkv_proj(x, m, attn_w_qkv[j].astype(BF16),
                                row(attn_q_norm[j] * (sm_scale * LOG2_E)),
                                row(attn_k_norm[j]), cos, sin, n_heads, n_kv, hd)
            o = _attention(q, k, v, n_kv, hd)
            x = _proj_norm(o, x, m, attn_w_out[j].astype(BF16), g, b, alpha)
        m = mods[2 * i + 1]
        x = _mlp_layer(x, m, mlp_w_in[i].astype(BF16), mlp_w_out[i].astype(BF16),
                       row(ln_g[i, 1]), row(ln_b[i, 1]), alpha)
    return x
```

```python
import functools

import jax
import jax.numpy as jnp
from jax import lax
from jax.experimental import pallas as pl
from jax.experimental.pallas import tpu as pltpu

F32 = jnp.float32
BF16 = jnp.bfloat16

GRID_W = 64
ROPE_THETA = 10000.0
LN_EPS = 1e-5
RMS_EPS = 1e-6
LOG2_E = 1.4426950408889634

V7X_VMEM_BYTES = 64 * 1024 * 1024
VMEM_LIMIT_BYTES = V7X_VMEM_BYTES - 8 * 1024 * 1024
LANES = 128
CONV_HALO_ROWS = 16


def _params(*semantics):
    return pltpu.CompilerParams(dimension_semantics=semantics, vmem_limit_bytes=VMEM_LIMIT_BYTES)


def _layer_norm(z, g, b):
    mu = jnp.mean(z, axis=-1, keepdims=True)
    zc = z - mu
    var = jnp.mean(zc * zc, axis=-1, keepdims=True)
    return zc * lax.rsqrt(var + LN_EPS) * g + b


def _modulate(x, m_ref):
    return x * (1.0 + m_ref[0, 1:2, :]) + m_ref[0, 0:1, :]


def _residual_norm(x, y, m_ref, g_ref, b_ref, alpha):
    z = alpha * x + (1.0 + m_ref[0, 2:3, :]) * y
    return _layer_norm(z, g_ref[...], b_ref[...])


def _mod_kernel(c_ref, w_ref, b_ref, o_ref):
    c = c_ref[...]
    c_act = (c * jax.nn.sigmoid(c)).astype(BF16)
    w = w_ref[0].astype(BF16)
    o_ref[0] = jnp.dot(c_act, w, preferred_element_type=F32) + b_ref[0]


def _modulation_all(c, mod_w, mod_b, tn=768):
    depth, two, d, d3 = mod_w.shape
    n_sub = depth * two
    bsz = c.shape[0]
    w = mod_w.reshape(n_sub, d, d3)
    b = mod_b.reshape(n_sub, 1, d3)
    out = pl.pallas_call(
        _mod_kernel,
        grid=(n_sub, d3 // tn),
        in_specs=[
            pl.BlockSpec((bsz, d), lambda l, n: (0, 0)),
            pl.BlockSpec((1, d, tn), lambda l, n: (l, 0, n)),
            pl.BlockSpec((1, 1, tn), lambda l, n: (l, 0, n)),
        ],
        out_specs=pl.BlockSpec((1, bsz, tn), lambda l, n: (l, 0, n)),
        out_shape=jax.ShapeDtypeStruct((n_sub, bsz, d3), F32),
        compiler_params=_params("arbitrary", "arbitrary"),
        name="modulation",
    )(c, w, b)
    return out.reshape(n_sub, bsz, 3, d)


def _mlp_kernel(x_ref, m_ref, win_ref, wout_ref, g_ref, b_ref, o_ref, h_ref, *, alpha):
    f = pl.program_id(2)

    @pl.when(f == 0)
    def _():
        h_ref[...] = _modulate(x_ref[0], m_ref).astype(BF16)
        o_ref[0] = jnp.zeros(o_ref.shape[1:], F32)

    u = jnp.dot(h_ref[...], win_ref[...], preferred_element_type=F32)
    u = jnp.maximum(u, 0.0)
    u = (u * u).astype(BF16)
    o_ref[0] += jnp.dot(u, wout_ref[...], preferred_element_type=F32)

    @pl.when(f == pl.num_programs(2) - 1)
    def _():
        o_ref[0] = _residual_norm(x_ref[0], o_ref[0], m_ref, g_ref, b_ref, alpha)


def _mlp_layer(x, mods, sub, w_in, w_out, layer, g, b, alpha, tm=1024, tf=512):
    bsz, seq, d = x.shape
    d_ff = w_in.shape[-1]
    return pl.pallas_call(
        functools.partial(_mlp_kernel, alpha=alpha),
        grid=(bsz, seq // tm, d_ff // tf),
        in_specs=[
            pl.BlockSpec((1, tm, d), lambda bi, i, f: (bi, i, 0)),
            pl.BlockSpec((None, 1, 3, d), lambda bi, i, f: (sub, bi, 0, 0)),
            pl.BlockSpec((None, d, tf), lambda bi, i, f: (layer, 0, f)),
            pl.BlockSpec((None, tf, d), lambda bi, i, f: (layer, f, 0)),
            pl.BlockSpec((1, d), lambda bi, i, f: (0, 0)),
            pl.BlockSpec((1, d), lambda bi, i, f: (0, 0)),
        ],
        out_specs=pl.BlockSpec((1, tm, d), lambda bi, i, f: (bi, i, 0)),
        out_shape=jax.ShapeDtypeStruct(x.shape, F32),
        scratch_shapes=[pltpu.VMEM((tm, d), BF16)],
        compiler_params=_params("parallel", "parallel", "arbitrary"),
        name="mlp",
    )(x, mods, w_in, w_out, g, b)


def _proj_norm_kernel(a_ref, x_ref, m_ref, w_ref, g_ref, b_ref, o_ref, *, alpha):
    y = jnp.dot(a_ref[0], w_ref[...], preferred_element_type=F32)
    o_ref[0] = _residual_norm(x_ref[0], y, m_ref, g_ref, b_ref, alpha)


def _proj_norm(a, x, mods, sub, w, layer, g, b, alpha, tm=512):
    bsz, seq, d = x.shape
    k = a.shape[-1]
    return pl.pallas_call(
        functools.partial(_proj_norm_kernel, alpha=alpha),
        grid=(bsz, seq // tm),
        in_specs=[
            pl.BlockSpec((1, tm, k), lambda bi, i: (bi, i, 0)),
            pl.BlockSpec((1, tm, d), lambda bi, i: (bi, i, 0)),
            pl.BlockSpec((None, 1, 3, d), lambda bi, i: (sub, bi, 0, 0)),
            pl.BlockSpec((None, k, d), lambda bi, i: (layer, 0, 0)),
            pl.BlockSpec((1, d), lambda bi, i: (0, 0)),
            pl.BlockSpec((1, d), lambda bi, i: (0, 0)),
        ],
        out_specs=pl.BlockSpec((1, tm, d), lambda bi, i: (bi, i, 0)),
        out_shape=jax.ShapeDtypeStruct(x.shape, F32),
        compiler_params=_params("parallel", "parallel"),
        name="proj_norm",
    )(a, x, mods, w, g, b)


def _qkv_kernel(x_ref, m_ref, w_ref, qg_ref, kg_ref, cos_ref, sin_ref, q_ref, k_ref, v_ref,
                *, n_heads, n_kv, hd):
    h = _modulate(x_ref[0], m_ref).astype(BF16)
    qkv = jnp.dot(h, w_ref[...], preferred_element_type=F32)
    cos = cos_ref[...]
    sin = sin_ref[...]

    def norm_rope(xh, gain):
        ms = jnp.mean(xh * xh, axis=-1, keepdims=True)
        y = xh * lax.rsqrt(ms + RMS_EPS) * gain
        return y * cos + pltpu.roll(y, hd // 2, 1) * sin

    for hi in range(n_heads):
        q_ref[0, :, hi * hd:(hi + 1) * hd] = norm_rope(
            qkv[:, hi * hd:(hi + 1) * hd], qg_ref[...]).astype(BF16)
    k0 = n_heads * hd
    for hi in range(n_kv):
        k_ref[0, :, hi * hd:(hi + 1) * hd] = norm_rope(
            qkv[:, k0 + hi * hd:k0 + (hi + 1) * hd], kg_ref[...]).astype(BF16)
    v0 = k0 + n_kv * hd
    v_ref[0] = qkv[:, v0:].astype(BF16)


def _qkv_proj(x, mods, sub, w_qkv, layer, q_gain, k_gain, cos, sin, n_heads, n_kv, hd, tm=512):
    bsz, seq, d = x.shape
    qkv_dim = w_qkv.shape[-1]
    return pl.pallas_call(
        functools.partial(_qkv_kernel, n_heads=n_heads, n_kv=n_kv, hd=hd),
        grid=(bsz, seq // tm),
        in_specs=[
            pl.BlockSpec((1, tm, d), lambda bi, i: (bi, i, 0)),
            pl.BlockSpec((None, 1, 3, d), lambda bi, i: (sub, bi, 0, 0)),
            pl.BlockSpec((None, d, qkv_dim), lambda bi, i: (layer, 0, 0)),
            pl.BlockSpec((1, hd), lambda bi, i: (0, 0)),
            pl.BlockSpec((1, hd), lambda bi, i: (0, 0)),
            pl.BlockSpec((tm, hd), lambda bi, i: (i, 0)),
            pl.BlockSpec((tm, hd), lambda bi, i: (i, 0)),
        ],
        out_specs=[
            pl.BlockSpec((1, tm, n_heads * hd), lambda bi, i: (bi, i, 0)),
            pl.BlockSpec((1, tm, n_kv * hd), lambda bi, i: (bi, i, 0)),
            pl.BlockSpec((1, tm, n_kv * hd), lambda bi, i: (bi, i, 0)),
        ],
        out_shape=[
            jax.ShapeDtypeStruct((bsz, seq, n_heads * hd), BF16),
            jax.ShapeDtypeStruct((bsz, seq, n_kv * hd), BF16),
            jax.ShapeDtypeStruct((bsz, seq, n_kv * hd), BF16),
        ],
        compiler_params=_params("parallel", "parallel"),
        name="qkv_proj",
    )(x, mods, w_qkv, q_gain, k_gain, cos, sin)


def _attn_kernel(q_ref, k_ref, v_ref, o_ref, s_ref, vt_ref, *, group, hd, tq):
    seq = k_ref.shape[1]
    n_tiles = seq // tq
    half = seq // 2
    vt_ref[...] = v_ref[0].T

    def logits_t(buf, row0, gi):
        q = q_ref[0, pl.ds(row0, tq), gi * hd:(gi + 1) * hd]
        for h0 in (0, half):
            s_ref[buf, h0:h0 + half, :] = lax.dot_general(
                k_ref[0, h0:h0 + half, :], q, (((1,), (1,)), ((), ())),
                preferred_element_type=F32)

    logits_t(0, 0, 0)

    def tile(it, carry):
        row0 = pl.multiple_of(it * tq, tq)
        row_next = pl.multiple_of(jnp.minimum(it + 1, n_tiles - 1) * tq, tq)
        for gi in range(group):
            cur = gi % 2
            if gi + 1 < group:
                logits_t(1 - cur, row0, gi + 1)
            else:
                logits_t(1 - cur, row_next, 0)
            s = s_ref[cur]
            p = jnp.exp2(s - jnp.max(s, axis=0, keepdims=True))
            denom = jnp.sum(p, axis=0, keepdims=True)
            ot = jnp.dot(vt_ref[...], p.astype(BF16), preferred_element_type=F32)
            o_ref[0, pl.ds(row0, tq), gi * hd:(gi + 1) * hd] = (ot / denom).T.astype(BF16)
        return carry

    lax.fori_loop(0, n_tiles, tile, 0)


def _attention(q, k, v, n_kv, hd, tq=256):
    bsz, seq, qd = q.shape
    group = qd // (n_kv * hd)
    assert group % 2 == 0, "logit buffers alternate per head and must realign every query tile"
    gw = group * hd
    return pl.pallas_call(
        functools.partial(_attn_kernel, group=group, hd=hd, tq=tq),
        grid=(bsz, n_kv),
        in_specs=[
            pl.BlockSpec((1, seq, gw), lambda bi, kv: (bi, 0, kv)),
            pl.BlockSpec((1, seq, hd), lambda bi, kv: (bi, 0, kv)),
            pl.BlockSpec((1, seq, hd), lambda bi, kv: (bi, 0, kv)),
        ],
        out_specs=pl.BlockSpec((1, seq, gw), lambda bi, kv: (bi, 0, kv)),
        out_shape=jax.ShapeDtypeStruct(q.shape, BF16),
        scratch_shapes=[pltpu.VMEM((2, seq, tq), F32), pltpu.VMEM((hd, seq), BF16)],
        compiler_params=_params("parallel", "parallel"),
        name="attention",
    )(q, k, v)


def _conv_layer_kernel(x_ref, xp_ref, xn_ref, m_ref, win_ref, bin_ref, dw_ref, dwb_ref, cg_ref,
                       cb_ref, wout_ref, g_ref, b_ref, o_ref, h_ref, buf_ref, c_ref,
                       *, alpha, width, rows, chunk):
    i = pl.program_id(1)
    ts, d = x_ref.shape[1:]
    halo = CONV_HALO_ROWS
    pad = width // 2
    first = i == 0
    last = i == pl.num_programs(1) - 1

    h_ref[0:halo, :] = _modulate(xp_ref[0], m_ref).astype(BF16)
    h_ref[halo:halo + ts, :] = _modulate(x_ref[0], m_ref).astype(BF16)
    h_ref[halo + ts:, :] = _modulate(xn_ref[0], m_ref).astype(BF16)

    for c0 in range(0, d, chunk):
        h = h_ref[...]
        a = jnp.dot(h, win_ref[:, c0:c0 + chunk], preferred_element_type=F32) + bin_ref[:, c0:c0 + chunk]
        gate = jnp.dot(h, win_ref[:, d + c0:d + c0 + chunk],
                       preferred_element_type=F32) + bin_ref[:, d + c0:d + c0 + chunk]
        glu = a * jax.nn.sigmoid(gate)
        for l0 in range(0, chunk, LANES):
            j = (c0 + l0) // LANES
            buf_ref[j, 0:halo, :] = jnp.where(first, 0.0, glu[0:halo, l0:l0 + LANES])
            buf_ref[j, halo:halo + ts, :] = glu[halo:halo + ts, l0:l0 + LANES]
            buf_ref[j, halo + ts:, :] = jnp.where(last, 0.0, glu[halo + ts:, l0:l0 + LANES])
        for l0 in range(0, chunk, LANES):
            j = (c0 + l0) // LANES
            for r0 in range(0, ts, rows):
                acc = jnp.zeros((rows, LANES), F32)
                for t in range(width):
                    tap = buf_ref[j, r0 + halo - pad + t:r0 + halo - pad + t + rows, :]
                    acc = acc + tap * dw_ref[j, t:t + 1, :]
                c_ref[r0:r0 + rows, c0 + l0:c0 + l0 + LANES] = acc

    v = _layer_norm(c_ref[...] + dwb_ref[...], cg_ref[...], cb_ref[...])
    v = (v * jax.nn.sigmoid(v)).astype(BF16)
    y = jnp.dot(v, wout_ref[...], preferred_element_type=F32)
    o_ref[0] = _residual_norm(x_ref[0], y, m_ref, g_ref, b_ref, alpha)


def _conv_layer(x, mods, sub, w_in, w_out, layer, b_in, dw, dw_b, cg, cb, g, b, alpha,
                ts=512, rows=64, chunk=512):
    bsz, seq, d = x.shape
    width = dw.shape[0]
    halo = CONV_HALO_ROWS
    assert width // 2 <= halo and ts % halo == 0 and d % chunk == 0 and chunk % LANES == 0
    nl = d // LANES
    per = ts // halo
    n_halo = seq // halo
    dw3 = dw.reshape(width, nl, LANES).transpose(1, 0, 2)
    const = lambda bi, i: (0, 0)
    resident = pl.Buffered(1)
    return pl.pallas_call(
        functools.partial(_conv_layer_kernel, alpha=alpha, width=width, rows=rows, chunk=chunk),
        grid=(bsz, seq // ts),
        in_specs=[
            pl.BlockSpec((1, ts, d), lambda bi, i: (bi, i, 0)),
            pl.BlockSpec((1, halo, d), lambda bi, i: (bi, jnp.maximum(i * per - 1, 0), 0)),
            pl.BlockSpec((1, halo, d), lambda bi, i: (bi, jnp.minimum((i + 1) * per, n_halo - 1), 0)),
            pl.BlockSpec((None, 1, 3, d), lambda bi, i: (sub, bi, 0, 0)),
            pl.BlockSpec((None, d, 2 * d), lambda bi, i: (layer, 0, 0), pipeline_mode=resident),
            pl.BlockSpec((1, 2 * d), const),
            pl.BlockSpec((nl, width, LANES), lambda bi, i: (0, 0, 0)),
            pl.BlockSpec((1, d), const),
            pl.BlockSpec((1, d), const),
            pl.BlockSpec((1, d), const),
            pl.BlockSpec((None, d, d), lambda bi, i: (layer, 0, 0), pipeline_mode=resident),
            pl.BlockSpec((1, d), const),
            pl.BlockSpec((1, d), const),
        ],
        out_specs=pl.BlockSpec((1, ts, d), lambda bi, i: (bi, i, 0)),
        out_shape=jax.ShapeDtypeStruct(x.shape, F32),
        scratch_shapes=[
            pltpu.VMEM((ts + 2 * halo, d), BF16),
            pltpu.VMEM((nl, ts + 2 * halo, LANES), F32),
            pltpu.VMEM((ts, d), F32),
        ],
        compiler_params=_params("parallel", "parallel"),
        name="conv_layer",
    )(x, x, x, mods, w_in, b_in, dw3, dw_b, cg, cb, w_out, g, b)


def _rope_tables(seq, hd):
    axis_dim = hd // 2
    half = axis_dim // 2
    t = jnp.arange(seq, dtype=jnp.int32)
    inv_freq = ROPE_THETA ** (-jnp.arange(half, dtype=F32) / half)
    ang_r = (t // GRID_W).astype(F32)[:, None] * inv_freq[None, :]
    ang_c = (t % GRID_W).astype(F32)[:, None] * inv_freq[None, :]
    cos_r, sin_r, cos_c, sin_c = jnp.cos(ang_r), jnp.sin(ang_r), jnp.cos(ang_c), jnp.sin(ang_c)
    cos = jnp.concatenate([cos_r, cos_c, cos_r, cos_c], axis=-1)
    sin = jnp.concatenate([-sin_r, -sin_c, sin_r, sin_c], axis=-1)
    return cos, sin


def _pair_lanes(a, hd):
    lead = a.shape[:-1]
    quarter = hd // 4
    a = a.reshape(*lead, -1, 2, 2, quarter)
    return jnp.swapaxes(a, -3, -2).reshape(*lead, -1)


def kernel(x, c, mod_w, mod_b, ln_g, ln_b, conv_w_in, conv_b_in, conv_dw, conv_dw_b, conv_ln_g,
           conv_ln_b, conv_w_out, attn_w_qkv, attn_q_norm, attn_k_norm, attn_w_out, mlp_w_in,
           mlp_w_out):
    depth = mod_w.shape[0]
    n_mixers = 2
    bsz, seq, d = x.shape
    hd = attn_q_norm.shape[-1]
    n_heads = attn_w_out.shape[1] // hd
    n_kv = (attn_w_qkv.shape[-1] - n_heads * hd) // (2 * hd)
    alpha = (2 * depth) ** 0.25
    sm_scale = hd ** -0.5

    mods = _modulation_all(c, mod_w, mod_b)
    cos, sin = _rope_tables(seq, hd)
    row = lambda a: a.reshape(1, -1)

    qk_cols = (n_heads + n_kv) * hd
    w_qkv = jnp.concatenate([_pair_lanes(attn_w_qkv[..., :qk_cols], hd), attn_w_qkv[..., qk_cols:]],
                            axis=-1).astype(BF16)
    q_gain = _pair_lanes(attn_q_norm, hd) * (sm_scale * LOG2_E)
    k_gain = _pair_lanes(attn_k_norm, hd)
    w_attn_out = attn_w_out.astype(BF16)
    w_conv_in, w_conv_out = conv_w_in.astype(BF16), conv_w_out.astype(BF16)
    w_mlp_in, w_mlp_out = mlp_w_in.astype(BF16), mlp_w_out.astype(BF16)

    for i in range(depth):
        j = i // n_mixers
        g, b = row(ln_g[i, 0]), row(ln_b[i, 0])
        if i % n_mixers == 0:
            x = _conv_layer(x, mods, 2 * i, w_conv_in, w_conv_out, j, row(conv_b_in[j]), conv_dw[j],
                            row(conv_dw_b[j]), row(conv_ln_g[j]), row(conv_ln_b[j]), g, b, alpha)
        else:
            q, k, v = _qkv_proj(x, mods, 2 * i, w_qkv, j, row(q_gain[j]), row(k_gain[j]), cos, sin,
                                n_heads, n_kv, hd)
            o = _attention(q, k, v, n_kv, hd)
            x = _proj_norm(o, x, mods, 2 * i, w_attn_out, j, g, b, alpha)
        x = _mlp_layer(x, mods, 2 * i + 1, w_mlp_in, w_mlp_out, i, row(ln_g[i, 1]), row(ln_b[i, 1]),
                       alpha)
    return x
```

```python
import functools

import jax
import jax.numpy as jnp
from jax import lax
from jax.experimental import pallas as pl
from jax.experimental.pallas import tpu as pltpu

F32 = jnp.float32
BF16 = jnp.bfloat16

GRID_W = 64
ROPE_THETA = 10000.0
LN_EPS = 1e-5
RMS_EPS = 1e-6
LOG2_E = 1.4426950408889634

V7X_VMEM_BYTES = 64 * 1024 * 1024
VMEM_LIMIT_BYTES = V7X_VMEM_BYTES - 8 * 1024 * 1024
LANES = 128
CONV_HALO_ROWS = 16


def _params(*semantics):
    return pltpu.CompilerParams(dimension_semantics=semantics, vmem_limit_bytes=VMEM_LIMIT_BYTES)


def _layer_norm(z, g, b):
    mu = jnp.mean(z, axis=-1, keepdims=True)
    zc = z - mu
    var = jnp.mean(zc * zc, axis=-1, keepdims=True)
    return zc * lax.rsqrt(var + LN_EPS) * g + b


def _modulate(x, m_ref):
    return x * (1.0 + m_ref[0, 1:2, :]) + m_ref[0, 0:1, :]


def _residual_norm(x, y, m_ref, g_ref, b_ref, alpha):
    z = alpha * x + (1.0 + m_ref[0, 2:3, :]) * y
    return _layer_norm(z, g_ref[...], b_ref[...])


def _mod_kernel(c_ref, w_ref, b_ref, o_ref):
    c = c_ref[...]
    c_act = (c * jax.nn.sigmoid(c)).astype(BF16)
    w = w_ref[0].astype(BF16)
    o_ref[0] = jnp.dot(c_act, w, preferred_element_type=F32) + b_ref[0]


def _modulation_all(c, mod_w, mod_b, tn=768):
    depth, two, d, d3 = mod_w.shape
    n_sub = depth * two
    bsz = c.shape[0]
    w = mod_w.reshape(n_sub, d, d3)
    b = mod_b.reshape(n_sub, 1, d3)
    out = pl.pallas_call(
        _mod_kernel,
        grid=(n_sub, d3 // tn),
        in_specs=[
            pl.BlockSpec((bsz, d), lambda l, n: (0, 0)),
            pl.BlockSpec((1, d, tn), lambda l, n: (l, 0, n)),
            pl.BlockSpec((1, 1, tn), lambda l, n: (l, 0, n)),
        ],
        out_specs=pl.BlockSpec((1, bsz, tn), lambda l, n: (l, 0, n)),
        out_shape=jax.ShapeDtypeStruct((n_sub, bsz, d3), F32),
        compiler_params=_params("arbitrary", "arbitrary"),
        name="modulation",
    )(c, w, b)
    return out.reshape(n_sub, bsz, 3, d)


def _mlp_kernel(x_ref, m_ref, win_ref, wout_ref, g_ref, b_ref, o_ref, h_ref, *, alpha, rb):
    f = pl.program_id(2)
    last = pl.num_programs(2) - 1
    tm = x_ref.shape[1]

    def hidden(h):
        u = jnp.maximum(jnp.dot(h, win_ref[...], preferred_element_type=F32), 0.0)
        return jnp.dot((u * u).astype(BF16), wout_ref[...], preferred_element_type=F32)

    @pl.when(f == 0)
    def _():
        for r0 in range(0, tm, rb):
            h = _modulate(x_ref[0, r0:r0 + rb, :], m_ref).astype(BF16)
            h_ref[r0:r0 + rb, :] = h
            o_ref[0, r0:r0 + rb, :] = hidden(h)

    @pl.when(jnp.logical_and(f > 0, f < last))
    def _():
        o_ref[0] += hidden(h_ref[...])

    @pl.when(f == last)
    def _():
        for r0 in range(0, tm, rb):
            y = o_ref[0, r0:r0 + rb, :] + hidden(h_ref[r0:r0 + rb, :])
            o_ref[0, r0:r0 + rb, :] = _residual_norm(x_ref[0, r0:r0 + rb, :], y, m_ref, g_ref,
                                                     b_ref, alpha)


def _mlp_layer(x, mods, sub, w_in, w_out, layer, g, b, alpha, tm=1024, tf=512, rb=256):
    bsz, seq, d = x.shape
    d_ff = w_in.shape[-1]
    assert d_ff // tf >= 2 and tm % rb == 0
    return pl.pallas_call(
        functools.partial(_mlp_kernel, alpha=alpha, rb=rb),
        grid=(bsz, seq // tm, d_ff // tf),
        in_specs=[
            pl.BlockSpec((1, tm, d), lambda bi, i, f: (bi, i, 0)),
            pl.BlockSpec((None, 1, 3, d), lambda bi, i, f: (sub, bi, 0, 0)),
            pl.BlockSpec((None, d, tf), lambda bi, i, f: (layer, 0, f)),
            pl.BlockSpec((None, tf, d), lambda bi, i, f: (layer, f, 0)),
            pl.BlockSpec((1, d), lambda bi, i, f: (0, 0)),
            pl.BlockSpec((1, d), lambda bi, i, f: (0, 0)),
        ],
        out_specs=pl.BlockSpec((1, tm, d), lambda bi, i, f: (bi, i, 0)),
        out_shape=jax.ShapeDtypeStruct(x.shape, F32),
        scratch_shapes=[pltpu.VMEM((tm, d), BF16)],
        compiler_params=_params("parallel", "parallel", "arbitrary"),
        name="mlp",
    )(x, mods, w_in, w_out, g, b)


def _proj_norm_kernel(a_ref, x_ref, m_ref, w_ref, g_ref, b_ref, o_ref, *, alpha, rb):
    for r0 in range(0, x_ref.shape[1], rb):
        y = jnp.dot(a_ref[0, r0:r0 + rb, :], w_ref[...], preferred_element_type=F32)
        o_ref[0, r0:r0 + rb, :] = _residual_norm(x_ref[0, r0:r0 + rb, :], y, m_ref, g_ref, b_ref,
                                                 alpha)


def _proj_norm(a, x, mods, sub, w, layer, g, b, alpha, tm=512, rb=256):
    bsz, seq, d = x.shape
    k = a.shape[-1]
    assert tm % rb == 0
    return pl.pallas_call(
        functools.partial(_proj_norm_kernel, alpha=alpha, rb=rb),
        grid=(bsz, seq // tm),
        in_specs=[
            pl.BlockSpec((1, tm, k), lambda bi, i: (bi, i, 0)),
            pl.BlockSpec((1, tm, d), lambda bi, i: (bi, i, 0)),
            pl.BlockSpec((None, 1, 3, d), lambda bi, i: (sub, bi, 0, 0)),
            pl.BlockSpec((None, k, d), lambda bi, i: (layer, 0, 0)),
            pl.BlockSpec((1, d), lambda bi, i: (0, 0)),
            pl.BlockSpec((1, d), lambda bi, i: (0, 0)),
        ],
        out_specs=pl.BlockSpec((1, tm, d), lambda bi, i: (bi, i, 0)),
        out_shape=jax.ShapeDtypeStruct(x.shape, F32),
        compiler_params=_params("parallel", "parallel"),
        name="proj_norm",
    )(a, x, mods, w, g, b)


def _qkv_kernel(x_ref, m_ref, w_ref, qg_ref, kg_ref, cos_ref, sin_ref, q_ref, k_ref, v_ref,
                *, n_heads, n_kv, hd):
    h = _modulate(x_ref[0], m_ref).astype(BF16)
    qkv = jnp.dot(h, w_ref[...], preferred_element_type=F32)
    cos = cos_ref[...]
    sin = sin_ref[...]

    def norm_rope(xh, gain):
        ms = jnp.mean(xh * xh, axis=-1, keepdims=True)
        y = xh * lax.rsqrt(ms + RMS_EPS) * gain
        return y * cos + pltpu.roll(y, hd // 2, 1) * sin

    for hi in range(n_heads):
        q_ref[0, :, hi * hd:(hi + 1) * hd] = norm_rope(
            qkv[:, hi * hd:(hi + 1) * hd], qg_ref[...]).astype(BF16)
    k0 = n_heads * hd
    for hi in range(n_kv):
        k_ref[0, :, hi * hd:(hi + 1) * hd] = norm_rope(
            qkv[:, k0 + hi * hd:k0 + (hi + 1) * hd], kg_ref[...]).astype(BF16)
    v0 = k0 + n_kv * hd
    v_ref[0] = qkv[:, v0:].astype(BF16)


def _qkv_proj(x, mods, sub, w_qkv, layer, q_gain, k_gain, cos, sin, n_heads, n_kv, hd, tm=512):
    bsz, seq, d = x.shape
    qkv_dim = w_qkv.shape[-1]
    return pl.pallas_call(
        functools.partial(_qkv_kernel, n_heads=n_heads, n_kv=n_kv, hd=hd),
        grid=(bsz, seq // tm),
        in_specs=[
            pl.BlockSpec((1, tm, d), lambda bi, i: (bi, i, 0)),
            pl.BlockSpec((None, 1, 3, d), lambda bi, i: (sub, bi, 0, 0)),
            pl.BlockSpec((None, d, qkv_dim), lambda bi, i: (layer, 0, 0)),
            pl.BlockSpec((1, hd), lambda bi, i: (0, 0)),
            pl.BlockSpec((1, hd), lambda bi, i: (0, 0)),
            pl.BlockSpec((tm, hd), lambda bi, i: (i, 0)),
            pl.BlockSpec((tm, hd), lambda bi, i: (i, 0)),
        ],
        out_specs=[
            pl.BlockSpec((1, tm, n_heads * hd), lambda bi, i: (bi, i, 0)),
            pl.BlockSpec((1, tm, n_kv * hd), lambda bi, i: (bi, i, 0)),
            pl.BlockSpec((1, tm, n_kv * hd), lambda bi, i: (bi, i, 0)),
        ],
        out_shape=[
            jax.ShapeDtypeStruct((bsz, seq, n_heads * hd), BF16),
            jax.ShapeDtypeStruct((bsz, seq, n_kv * hd), BF16),
            jax.ShapeDtypeStruct((bsz, seq, n_kv * hd), BF16),
        ],
        compiler_params=_params("parallel", "parallel"),
        name="qkv_proj",
    )(x, mods, w_qkv, q_gain, k_gain, cos, sin)


def _attn_kernel(q_ref, k_ref, v_ref, o_ref, s_ref, vt_ref, *, group, hd, tq, unroll):
    seq = k_ref.shape[1]
    n_tiles = seq // tq
    half = seq // 2
    vt_ref[...] = v_ref[0].T

    def logits_t(buf, row0, gi):
        q = q_ref[0, pl.ds(row0, tq), gi * hd:(gi + 1) * hd]
        for h0 in (0, half):
            s_ref[buf, h0:h0 + half, :] = lax.dot_general(
                k_ref[0, h0:h0 + half, :], q, (((1,), (1,)), ((), ())),
                preferred_element_type=F32)

    logits_t(0, 0, 0)

    def tiles(it, carry):
        for sub in range(unroll):
            row0 = pl.multiple_of((it * unroll + sub) * tq, tq)
            row_next = pl.multiple_of(jnp.minimum(it * unroll + sub + 1, n_tiles - 1) * tq, tq)
            for gi in range(group):
                cur = gi % 2
                if gi + 1 < group:
                    logits_t(1 - cur, row0, gi + 1)
                else:
                    logits_t(1 - cur, row_next, 0)
                s = s_ref[cur]
                p = jnp.exp2(s - jnp.max(s, axis=0, keepdims=True))
                denom = jnp.sum(p, axis=0, keepdims=True)
                ot = jnp.dot(vt_ref[...], p.astype(BF16), preferred_element_type=F32)
                o_ref[0, pl.ds(row0, tq), gi * hd:(gi + 1) * hd] = (ot / denom).T.astype(BF16)
        return carry

    lax.fori_loop(0, n_tiles // unroll, tiles, 0)


def _attention(q, k, v, n_kv, hd, tq=256, unroll=4):
    bsz, seq, qd = q.shape
    group = qd // (n_kv * hd)
    assert group % 2 == 0, "logit buffers alternate per head and must realign every query tile"
    assert (seq // tq) % unroll == 0
    gw = group * hd
    return pl.pallas_call(
        functools.partial(_attn_kernel, group=group, hd=hd, tq=tq, unroll=unroll),
        grid=(bsz, n_kv),
        in_specs=[
            pl.BlockSpec((1, seq, gw), lambda bi, kv: (bi, 0, kv)),
            pl.BlockSpec((1, seq, hd), lambda bi, kv: (bi, 0, kv)),
            pl.BlockSpec((1, seq, hd), lambda bi, kv: (bi, 0, kv)),
        ],
        out_specs=pl.BlockSpec((1, seq, gw), lambda bi, kv: (bi, 0, kv)),
        out_shape=jax.ShapeDtypeStruct(q.shape, BF16),
        scratch_shapes=[pltpu.VMEM((2, seq, tq), F32), pltpu.VMEM((hd, seq), BF16)],
        compiler_params=_params("parallel", "parallel"),
        name="attention",
    )(q, k, v)


def _conv_layer_kernel(x_ref, xp_ref, xn_ref, m_ref, win_ref, bin_ref, dw_ref, dwb_ref, cg_ref,
                       cb_ref, wout_ref, g_ref, b_ref, o_ref, h_ref, buf_ref, c_ref,
                       *, alpha, width, rows, chunk):
    i = pl.program_id(1)
    ts, d = x_ref.shape[1:]
    halo = CONV_HALO_ROWS
    pad = width // 2
    first = i == 0
    last = i == pl.num_programs(1) - 1

    h_ref[0:halo, :] = _modulate(xp_ref[0], m_ref).astype(BF16)
    h_ref[halo:halo + ts, :] = _modulate(x_ref[0], m_ref).astype(BF16)
    h_ref[halo + ts:, :] = _modulate(xn_ref[0], m_ref).astype(BF16)

    for c0 in range(0, d, chunk):
        h = h_ref[...]
        a = jnp.dot(h, win_ref[:, c0:c0 + chunk], preferred_element_type=F32) + bin_ref[:, c0:c0 + chunk]
        gate = jnp.dot(h, win_ref[:, d + c0:d + c0 + chunk],
                       preferred_element_type=F32) + bin_ref[:, d + c0:d + c0 + chunk]
        glu = a * jax.nn.sigmoid(gate)
        for l0 in range(0, chunk, LANES):
            j = (c0 + l0) // LANES
            buf_ref[j, 0:halo, :] = jnp.where(first, 0.0, glu[0:halo, l0:l0 + LANES])
            buf_ref[j, halo:halo + ts, :] = glu[halo:halo + ts, l0:l0 + LANES]
            buf_ref[j, halo + ts:, :] = jnp.where(last, 0.0, glu[halo + ts:, l0:l0 + LANES])
        for l0 in range(0, chunk, LANES):
            j = (c0 + l0) // LANES
            for r0 in range(0, ts, rows):
                acc = jnp.zeros((rows, LANES), F32)
                for t in range(width):
                    tap = buf_ref[j, r0 + halo - pad + t:r0 + halo - pad + t + rows, :]
                    acc = acc + tap * dw_ref[j, t:t + 1, :]
                c_ref[r0:r0 + rows, c0 + l0:c0 + l0 + LANES] = acc

    v = _layer_norm(c_ref[...] + dwb_ref[...], cg_ref[...], cb_ref[...])
    v = (v * jax.nn.sigmoid(v)).astype(BF16)
    y = jnp.dot(v, wout_ref[...], preferred_element_type=F32)
    o_ref[0] = _residual_norm(x_ref[0], y, m_ref, g_ref, b_ref, alpha)


def _conv_layer(x, mods, sub, w_in, w_out, layer, b_in, dw, dw_b, cg, cb, g, b, alpha,
                ts=512, rows=64, chunk=512):
    bsz, seq, d = x.shape
    width = dw.shape[0]
    halo = CONV_HALO_ROWS
    assert width // 2 <= halo and ts % halo == 0 and d % chunk == 0 and chunk % LANES == 0
    nl = d // LANES
    per = ts // halo
    n_halo = seq // halo
    dw3 = dw.reshape(width, nl, LANES).transpose(1, 0, 2)
    const = lambda bi, i: (0, 0)
    resident = pl.Buffered(1)
    return pl.pallas_call(
        functools.partial(_conv_layer_kernel, alpha=alpha, width=width, rows=rows, chunk=chunk),
        grid=(bsz, seq // ts),
        in_specs=[
            pl.BlockSpec((1, ts, d), lambda bi, i: (bi, i, 0)),
            pl.BlockSpec((1, halo, d), lambda bi, i: (bi, jnp.maximum(i * per - 1, 0), 0)),
            pl.BlockSpec((1, halo, d), lambda bi, i: (bi, jnp.minimum((i + 1) * per, n_halo - 1), 0)),
            pl.BlockSpec((None, 1, 3, d), lambda bi, i: (sub, bi, 0, 0)),
            pl.BlockSpec((None, d, 2 * d), lambda bi, i: (layer, 0, 0), pipeline_mode=resident),
            pl.BlockSpec((1, 2 * d), const),
            pl.BlockSpec((nl, width, LANES), lambda bi, i: (0, 0, 0)),
            pl.BlockSpec((1, d), const),
            pl.BlockSpec((1, d), const),
            pl.BlockSpec((1, d), const),
            pl.BlockSpec((None, d, d), lambda bi, i: (layer, 0, 0), pipeline_mode=resident),
            pl.BlockSpec((1, d), const),
            pl.BlockSpec((1, d), const),
        ],
        out_specs=pl.BlockSpec((1, ts, d), lambda bi, i: (bi, i, 0)),
        out_shape=jax.ShapeDtypeStruct(x.shape, F32),
        scratch_shapes=[
            pltpu.VMEM((ts + 2 * halo, d), BF16),
            pltpu.VMEM((nl, ts + 2 * halo, LANES), F32),
            pltpu.VMEM((ts, d), F32),
        ],
        compiler_params=_params("parallel", "parallel"),
        name="conv_layer",
    )(x, x, x, mods, w_in, b_in, dw3, dw_b, cg, cb, w_out, g, b)


def _rope_tables(seq, hd):
    axis_dim = hd // 2
    half = axis_dim // 2
    t = jnp.arange(seq, dtype=jnp.int32)
    inv_freq = ROPE_THETA ** (-jnp.arange(half, dtype=F32) / half)
    ang_r = (t // GRID_W).astype(F32)[:, None] * inv_freq[None, :]
    ang_c = (t % GRID_W).astype(F32)[:, None] * inv_freq[None, :]
    cos_r, sin_r, cos_c, sin_c = jnp.cos(ang_r), jnp.sin(ang_r), jnp.cos(ang_c), jnp.sin(ang_c)
    cos = jnp.concatenate([cos_r, cos_c, cos_r, cos_c], axis=-1)
    sin = jnp.concatenate([-sin_r, -sin_c, sin_r, sin_c], axis=-1)
    return cos, sin


def _pair_lanes(a, hd):
    lead = a.shape[:-1]
    quarter = hd // 4
    a = a.reshape(*lead, -1, 2, 2, quarter)
    return jnp.swapaxes(a, -3, -2).reshape(*lead, -1)


def kernel(x, c, mod_w, mod_b, ln_g, ln_b, conv_w_in, conv_b_in, conv_dw, conv_dw_b, conv_ln_g,
           conv_ln_b, conv_w_out, attn_w_qkv, attn_q_norm, attn_k_norm, attn_w_out, mlp_w_in,
           mlp_w_out):
    depth = mod_w.shape[0]
    n_mixers = 2
    bsz, seq, d = x.shape
    hd = attn_q_norm.shape[-1]
    n_heads = attn_w_out.shape[1] // hd
    n_kv = (attn_w_qkv.shape[-1] - n_heads * hd) // (2 * hd)
    alpha = (2 * depth) ** 0.25
    sm_scale = hd ** -0.5

    mods = _modulation_all(c, mod_w, mod_b)
    cos, sin = _rope_tables(seq, hd)
    row = lambda a: a.reshape(1, -1)

    qk_cols = (n_heads + n_kv) * hd
    w_qkv = jnp.concatenate([_pair_lanes(attn_w_qkv[..., :qk_cols], hd), attn_w_qkv[..., qk_cols:]],
                            axis=-1).astype(BF16)
    q_gain = _pair_lanes(attn_q_norm, hd) * (sm_scale * LOG2_E)
    k_gain = _pair_lanes(attn_k_norm, hd)
    w_attn_out = attn_w_out.astype(BF16)
    w_conv_in, w_conv_out = conv_w_in.astype(BF16), conv_w_out.astype(BF16)
    w_mlp_in, w_mlp_out = mlp_w_in.astype(BF16), mlp_w_out.astype(BF16)

    for i in range(depth):
        j = i // n_mixers
        g, b = row(ln_g[i, 0]), row(ln_b[i, 0])
        if i % n_mixers == 0:
            x = _conv_layer(x, mods, 2 * i, w_conv_in, w_conv_out, j, row(conv_b_in[j]), conv_dw[j],
                            row(conv_dw_b[j]), row(conv_ln_g[j]), row(conv_ln_b[j]), g, b, alpha)
        else:
            q, k, v = _qkv_proj(x, mods, 2 * i, w_qkv, j, row(q_gain[j]), row(k_gain[j]), cos, sin,
                                n_heads, n_kv, hd)
            o = _attention(q, k, v, n_kv, hd)
            x = _proj_norm(o, x, mods, 2 * i, w_attn_out, j, g, b, alpha)
        x = _mlp_layer(x, mods, 2 * i + 1, w_mlp_in, w_mlp_out, i, row(ln_g[i, 1]), row(ln_b[i, 1]),
                       alpha)
    return x
```

```python
import functools

import jax
import jax.numpy as jnp
from jax import lax
from jax.experimental import pallas as pl
from jax.experimental.pallas import tpu as pltpu

F32 = jnp.float32
BF16 = jnp.bfloat16

GRID_W = 64
ROPE_THETA = 10000.0
LN_EPS = 1e-5
RMS_EPS = 1e-6
LOG2_E = 1.4426950408889634

V7X_VMEM_BYTES = 64 * 1024 * 1024
VMEM_LIMIT_BYTES = V7X_VMEM_BYTES - 8 * 1024 * 1024
LANES = 128
CONV_HALO_ROWS = 16
MLP_HIDDEN_CHUNK = 512


def _params(*semantics):
    return pltpu.CompilerParams(dimension_semantics=semantics, vmem_limit_bytes=VMEM_LIMIT_BYTES)


def _layer_norm(z, g, b):
    mu = jnp.mean(z, axis=-1, keepdims=True)
    zc = z - mu
    var = jnp.mean(zc * zc, axis=-1, keepdims=True)
    return zc * lax.rsqrt(var + LN_EPS) * g + b


def _modulate(x, m_ref):
    return x * (1.0 + m_ref[0, 1:2, :]) + m_ref[0, 0:1, :]


def _residual_norm(x, y, m_ref, g_ref, b_ref, alpha):
    z = alpha * x + (1.0 + m_ref[0, 2:3, :]) * y
    return _layer_norm(z, g_ref[...], b_ref[...])


def _mod_kernel(c_ref, w_ref, b_ref, o_ref):
    c = c_ref[...]
    c_act = (c * jax.nn.sigmoid(c)).astype(BF16)
    w = w_ref[0].astype(BF16)
    o_ref[0] = jnp.dot(c_act, w, preferred_element_type=F32) + b_ref[0]


def _modulation_all(c, mod_w, mod_b, tn=768):
    depth, two, d, d3 = mod_w.shape
    n_sub = depth * two
    bsz = c.shape[0]
    w = mod_w.reshape(n_sub, d, d3)
    b = mod_b.reshape(n_sub, 1, d3)
    out = pl.pallas_call(
        _mod_kernel,
        grid=(n_sub, d3 // tn),
        in_specs=[
            pl.BlockSpec((bsz, d), lambda l, n: (0, 0)),
            pl.BlockSpec((1, d, tn), lambda l, n: (l, 0, n)),
            pl.BlockSpec((1, 1, tn), lambda l, n: (l, 0, n)),
        ],
        out_specs=pl.BlockSpec((1, bsz, tn), lambda l, n: (l, 0, n)),
        out_shape=jax.ShapeDtypeStruct((n_sub, bsz, d3), F32),
        compiler_params=_params("arbitrary", "arbitrary"),
        name="modulation",
    )(c, w, b)
    return out.reshape(n_sub, bsz, 3, d)


def _mlp_kernel(x_ref, m_ref, win_ref, wout_ref, g_ref, b_ref, o_ref, h_ref, *, alpha, rb):
    f = pl.program_id(2)
    last = pl.num_programs(2) - 1
    tm = x_ref.shape[1]

    def hidden(h):
        u = jnp.maximum(jnp.dot(h, win_ref[...], preferred_element_type=F32), 0.0)
        return jnp.dot((u * u).astype(BF16), wout_ref[...], preferred_element_type=F32)

    @pl.when(f == 0)
    def _():
        for r0 in range(0, tm, rb):
            h = _modulate(x_ref[0, r0:r0 + rb, :], m_ref).astype(BF16)
            h_ref[r0:r0 + rb, :] = h
            o_ref[0, r0:r0 + rb, :] = hidden(h)

    @pl.when(jnp.logical_and(f > 0, f < last))
    def _():
        o_ref[0] += hidden(h_ref[...])

    @pl.when(f == last)
    def _():
        for r0 in range(0, tm, rb):
            y = o_ref[0, r0:r0 + rb, :] + hidden(h_ref[r0:r0 + rb, :])
            o_ref[0, r0:r0 + rb, :] = _residual_norm(x_ref[0, r0:r0 + rb, :], y, m_ref, g_ref,
                                                     b_ref, alpha)


def _mlp_layer(x, mods, sub, w_in, w_out, layer, g, b, alpha, tm=1024, rb=256):
    bsz, seq, d = x.shape
    n_f, tf = w_in.shape[1], w_in.shape[3]
    assert n_f >= 2 and tm % rb == 0
    return pl.pallas_call(
        functools.partial(_mlp_kernel, alpha=alpha, rb=rb),
        grid=(bsz, seq // tm, n_f),
        in_specs=[
            pl.BlockSpec((1, tm, d), lambda bi, i, f: (bi, i, 0)),
            pl.BlockSpec((None, 1, 3, d), lambda bi, i, f: (sub, bi, 0, 0)),
            pl.BlockSpec((None, None, d, tf), lambda bi, i, f: (layer, f, 0, 0)),
            pl.BlockSpec((None, tf, d), lambda bi, i, f: (layer, f, 0)),
            pl.BlockSpec((1, d), lambda bi, i, f: (0, 0)),
            pl.BlockSpec((1, d), lambda bi, i, f: (0, 0)),
        ],
        out_specs=pl.BlockSpec((1, tm, d), lambda bi, i, f: (bi, i, 0)),
        out_shape=jax.ShapeDtypeStruct(x.shape, F32),
        scratch_shapes=[pltpu.VMEM((tm, d), BF16)],
        compiler_params=_params("parallel", "parallel", "arbitrary"),
        name="mlp",
    )(x, mods, w_in, w_out, g, b)


def _proj_norm_kernel(a_ref, x_ref, m_ref, w_ref, g_ref, b_ref, o_ref, *, alpha, rb):
    for r0 in range(0, x_ref.shape[1], rb):
        y = jnp.dot(a_ref[0, r0:r0 + rb, :], w_ref[...], preferred_element_type=F32)
        o_ref[0, r0:r0 + rb, :] = _residual_norm(x_ref[0, r0:r0 + rb, :], y, m_ref, g_ref, b_ref,
                                                 alpha)


def _proj_norm(a, x, mods, sub, w, layer, g, b, alpha, tm=512, rb=256):
    bsz, seq, d = x.shape
    k = a.shape[-1]
    assert tm % rb == 0
    return pl.pallas_call(
        functools.partial(_proj_norm_kernel, alpha=alpha, rb=rb),
        grid=(bsz, seq // tm),
        in_specs=[
            pl.BlockSpec((1, tm, k), lambda bi, i: (bi, i, 0)),
            pl.BlockSpec((1, tm, d), lambda bi, i: (bi, i, 0)),
            pl.BlockSpec((None, 1, 3, d), lambda bi, i: (sub, bi, 0, 0)),
            pl.BlockSpec((None, k, d), lambda bi, i: (layer, 0, 0)),
            pl.BlockSpec((1, d), lambda bi, i: (0, 0)),
            pl.BlockSpec((1, d), lambda bi, i: (0, 0)),
        ],
        out_specs=pl.BlockSpec((1, tm, d), lambda bi, i: (bi, i, 0)),
        out_shape=jax.ShapeDtypeStruct(x.shape, F32),
        compiler_params=_params("parallel", "parallel"),
        name="proj_norm",
    )(a, x, mods, w, g, b)


def _qkv_kernel(x_ref, m_ref, w_ref, qg_ref, kg_ref, cos_ref, sin_ref, q_ref, k_ref, v_ref,
                *, n_heads, n_kv, hd, rb):
    k0 = n_heads * hd
    v0 = k0 + n_kv * hd
    for r0 in range(0, x_ref.shape[1], rb):
        rws = slice(r0, r0 + rb)
        h = _modulate(x_ref[0, rws, :], m_ref).astype(BF16)
        qkv = jnp.dot(h, w_ref[...], preferred_element_type=F32)
        cos = cos_ref[rws, :]
        sin = sin_ref[rws, :]

        def norm_rope(xh, gain):
            ms = jnp.mean(xh * xh, axis=-1, keepdims=True)
            y = xh * lax.rsqrt(ms + RMS_EPS) * gain
            return y * cos + pltpu.roll(y, hd // 2, 1) * sin

        for hi in range(n_heads):
            q_ref[0, rws, hi * hd:(hi + 1) * hd] = norm_rope(
                qkv[:, hi * hd:(hi + 1) * hd], qg_ref[...]).astype(BF16)
        for hi in range(n_kv):
            k_ref[0, rws, hi * hd:(hi + 1) * hd] = norm_rope(
                qkv[:, k0 + hi * hd:k0 + (hi + 1) * hd], kg_ref[...]).astype(BF16)
        v_ref[0, rws, :] = qkv[:, v0:].astype(BF16)


def _qkv_proj(x, mods, sub, w_qkv, layer, q_gain, k_gain, cos, sin, n_heads, n_kv, hd, tm=512,
              rb=256):
    bsz, seq, d = x.shape
    qkv_dim = w_qkv.shape[-1]
    assert tm % rb == 0
    return pl.pallas_call(
        functools.partial(_qkv_kernel, n_heads=n_heads, n_kv=n_kv, hd=hd, rb=rb),
        grid=(bsz, seq // tm),
        in_specs=[
            pl.BlockSpec((1, tm, d), lambda bi, i: (bi, i, 0)),
            pl.BlockSpec((None, 1, 3, d), lambda bi, i: (sub, bi, 0, 0)),
            pl.BlockSpec((None, d, qkv_dim), lambda bi, i: (layer, 0, 0)),
            pl.BlockSpec((1, hd), lambda bi, i: (0, 0)),
            pl.BlockSpec((1, hd), lambda bi, i: (0, 0)),
            pl.BlockSpec((tm, hd), lambda bi, i: (i, 0)),
            pl.BlockSpec((tm, hd), lambda bi, i: (i, 0)),
        ],
        out_specs=[
            pl.BlockSpec((1, tm, n_heads * hd), lambda bi, i: (bi, i, 0)),
            pl.BlockSpec((1, tm, n_kv * hd), lambda bi, i: (bi, i, 0)),
            pl.BlockSpec((1, tm, n_kv * hd), lambda bi, i: (bi, i, 0)),
        ],
        out_shape=[
            jax.ShapeDtypeStruct((bsz, seq, n_heads * hd), BF16),
            jax.ShapeDtypeStruct((bsz, seq, n_kv * hd), BF16),
            jax.ShapeDtypeStruct((bsz, seq, n_kv * hd), BF16),
        ],
        compiler_params=_params("parallel", "parallel"),
        name="qkv_proj",
    )(x, mods, w_qkv, q_gain, k_gain, cos, sin)


def _attn_kernel(q_ref, k_ref, v_ref, o_ref, s_ref, vt_ref, *, group, hd, tq, unroll):
    seq = k_ref.shape[1]
    n_tiles = seq // tq
    half = seq // 2
    vt_ref[...] = v_ref[0].T

    def logits_t(buf, row0, gi):
        q = q_ref[0, pl.ds(row0, tq), gi * hd:(gi + 1) * hd]
        for h0 in (0, half):
            s_ref[buf, h0:h0 + half, :] = lax.dot_general(
                k_ref[0, h0:h0 + half, :], q, (((1,), (1,)), ((), ())),
                preferred_element_type=F32)

    logits_t(0, 0, 0)

    def tiles(it, carry):
        for sub in range(unroll):
            row0 = pl.multiple_of((it * unroll + sub) * tq, tq)
            row_next = pl.multiple_of(jnp.minimum(it * unroll + sub + 1, n_tiles - 1) * tq, tq)
            for gi in range(group):
                cur = gi % 2
                if gi + 1 < group:
                    logits_t(1 - cur, row0, gi + 1)
                else:
                    logits_t(1 - cur, row_next, 0)
                s = s_ref[cur]
                p = jnp.exp2(s - jnp.max(s, axis=0, keepdims=True))
                denom = jnp.sum(p, axis=0, keepdims=True)
                ot = jnp.dot(vt_ref[...], p.astype(BF16), preferred_element_type=F32)
                o_ref[0, pl.ds(row0, tq), gi * hd:(gi + 1) * hd] = (ot / denom).T.astype(BF16)
        return carry

    lax.fori_loop(0, n_tiles // unroll, tiles, 0)


def _attention(q, k, v, n_kv, hd, tq=256, unroll=4):
    bsz, seq, qd = q.shape
    group = qd // (n_kv * hd)
    assert group % 2 == 0, "logit buffers alternate per head and must realign every query tile"
    assert (seq // tq) % unroll == 0
    gw = group * hd
    return pl.pallas_call(
        functools.partial(_attn_kernel, group=group, hd=hd, tq=tq, unroll=unroll),
        grid=(bsz, n_kv),
        in_specs=[
            pl.BlockSpec((1, seq, gw), lambda bi, kv: (bi, 0, kv)),
            pl.BlockSpec((1, seq, hd), lambda bi, kv: (bi, 0, kv)),
            pl.BlockSpec((1, seq, hd), lambda bi, kv: (bi, 0, kv)),
        ],
        out_specs=pl.BlockSpec((1, seq, gw), lambda bi, kv: (bi, 0, kv)),
        out_shape=jax.ShapeDtypeStruct(q.shape, BF16),
        scratch_shapes=[pltpu.VMEM((2, seq, tq), F32), pltpu.VMEM((hd, seq), BF16)],
        compiler_params=_params("parallel", "parallel"),
        name="attention",
    )(q, k, v)


def _conv_layer_kernel(x_ref, xp_ref, xn_ref, m_ref, win_ref, bin_ref, dw_ref, dwb_ref, cg_ref,
                       cb_ref, wout_ref, g_ref, b_ref, o_ref, h_ref, buf_ref, c_ref,
                       *, alpha, width, rows, chunk):
    i = pl.program_id(1)
    ts, d = x_ref.shape[1:]
    halo = CONV_HALO_ROWS
    pad = width // 2
    first = i == 0
    last = i == pl.num_programs(1) - 1

    h_ref[0:halo, :] = _modulate(xp_ref[0], m_ref).astype(BF16)
    h_ref[halo:halo + ts, :] = _modulate(x_ref[0], m_ref).astype(BF16)
    h_ref[halo + ts:, :] = _modulate(xn_ref[0], m_ref).astype(BF16)

    for c0 in range(0, d, chunk):
        h = h_ref[...]
        a = jnp.dot(h, win_ref[:, c0:c0 + chunk], preferred_element_type=F32) + bin_ref[:, c0:c0 + chunk]
        gate = jnp.dot(h, win_ref[:, d + c0:d + c0 + chunk],
                       preferred_element_type=F32) + bin_ref[:, d + c0:d + c0 + chunk]
        glu = a * jax.nn.sigmoid(gate)
        for l0 in range(0, chunk, LANES):
            j = (c0 + l0) // LANES
            buf_ref[j, 0:halo, :] = jnp.where(first, 0.0, glu[0:halo, l0:l0 + LANES])
            buf_ref[j, halo:halo + ts, :] = glu[halo:halo + ts, l0:l0 + LANES]
            buf_ref[j, halo + ts:, :] = jnp.where(last, 0.0, glu[halo + ts:, l0:l0 + LANES])
        for l0 in range(0, chunk, LANES):
            j = (c0 + l0) // LANES
            for r0 in range(0, ts, rows):
                acc = jnp.zeros((rows, LANES), F32)
                for t in range(width):
                    tap = buf_ref[j, r0 + halo - pad + t:r0 + halo - pad + t + rows, :]
                    acc = acc + tap * dw_ref[j, t:t + 1, :]
                c_ref[r0:r0 + rows, c0 + l0:c0 + l0 + LANES] = acc

    v = _layer_norm(c_ref[...] + dwb_ref[...], cg_ref[...], cb_ref[...])
    v = (v * jax.nn.sigmoid(v)).astype(BF16)
    y = jnp.dot(v, wout_ref[...], preferred_element_type=F32)
    o_ref[0] = _residual_norm(x_ref[0], y, m_ref, g_ref, b_ref, alpha)


def _conv_layer(x, mods, sub, w_in, w_out, layer, b_in, dw, dw_b, cg, cb, g, b, alpha,
                ts=512, rows=64, chunk=512):
    bsz, seq, d = x.shape
    width = dw.shape[0]
    halo = CONV_HALO_ROWS
    assert width // 2 <= halo and ts % halo == 0 and d % chunk == 0 and chunk % LANES == 0
    nl = d // LANES
    per = ts // halo
    n_halo = seq // halo
    dw3 = dw.reshape(width, nl, LANES).transpose(1, 0, 2)
    const = lambda bi, i: (0, 0)
    resident = pl.Buffered(1)
    return pl.pallas_call(
        functools.partial(_conv_layer_kernel, alpha=alpha, width=width, rows=rows, chunk=chunk),
        grid=(bsz, seq // ts),
        in_specs=[
            pl.BlockSpec((1, ts, d), lambda bi, i: (bi, i, 0)),
            pl.BlockSpec((1, halo, d), lambda bi, i: (bi, jnp.maximum(i * per - 1, 0), 0)),
            pl.BlockSpec((1, halo, d), lambda bi, i: (bi, jnp.minimum((i + 1) * per, n_halo - 1), 0)),
            pl.BlockSpec((None, 1, 3, d), lambda bi, i: (sub, bi, 0, 0)),
            pl.BlockSpec((None, d, 2 * d), lambda bi, i: (layer, 0, 0), pipeline_mode=resident),
            pl.BlockSpec((1, 2 * d), const),
            pl.BlockSpec((nl, width, LANES), lambda bi, i: (0, 0, 0)),
            pl.BlockSpec((1, d), const),
            pl.BlockSpec((1, d), const),
            pl.BlockSpec((1, d), const),
            pl.BlockSpec((None, d, d), lambda bi, i: (layer, 0, 0), pipeline_mode=resident),
            pl.BlockSpec((1, d), const),
            pl.BlockSpec((1, d), const),
        ],
        out_specs=pl.BlockSpec((1, ts, d), lambda bi, i: (bi, i, 0)),
        out_shape=jax.ShapeDtypeStruct(x.shape, F32),
        scratch_shapes=[
            pltpu.VMEM((ts + 2 * halo, d), BF16),
            pltpu.VMEM((nl, ts + 2 * halo, LANES), F32),
            pltpu.VMEM((ts, d), F32),
        ],
        compiler_params=_params("parallel", "parallel"),
        name="conv_layer",
    )(x, x, x, mods, w_in, b_in, dw3, dw_b, cg, cb, w_out, g, b)


def _rope_tables(seq, hd):
    axis_dim = hd // 2
    half = axis_dim // 2
    t = jnp.arange(seq, dtype=jnp.int32)
    inv_freq = ROPE_THETA ** (-jnp.arange(half, dtype=F32) / half)
    ang_r = (t // GRID_W).astype(F32)[:, None] * inv_freq[None, :]
    ang_c = (t % GRID_W).astype(F32)[:, None] * inv_freq[None, :]
    cos_r, sin_r, cos_c, sin_c = jnp.cos(ang_r), jnp.sin(ang_r), jnp.cos(ang_c), jnp.sin(ang_c)
    cos = jnp.concatenate([cos_r, cos_c, cos_r, cos_c], axis=-1)
    sin = jnp.concatenate([-sin_r, -sin_c, sin_r, sin_c], axis=-1)
    return cos, sin


def _pair_lanes(a, hd):
    lead = a.shape[:-1]
    quarter = hd // 4
    a = a.reshape(*lead, -1, 2, 2, quarter)
    return jnp.swapaxes(a, -3, -2).reshape(*lead, -1)


def kernel(x, c, mod_w, mod_b, ln_g, ln_b, conv_w_in, conv_b_in, conv_dw, conv_dw_b, conv_ln_g,
           conv_ln_b, conv_w_out, attn_w_qkv, attn_q_norm, attn_k_norm, attn_w_out, mlp_w_in,
           mlp_w_out):
    depth = mod_w.shape[0]
    n_mixers = 2
    bsz, seq, d = x.shape
    hd = attn_q_norm.shape[-1]
    n_heads = attn_w_out.shape[1] // hd
    n_kv = (attn_w_qkv.shape[-1] - n_heads * hd) // (2 * hd)
    alpha = (2 * depth) ** 0.25
    sm_scale = hd ** -0.5

    mods = _modulation_all(c, mod_w, mod_b)
    cos, sin = _rope_tables(seq, hd)
    row = lambda a: a.reshape(1, -1)

    qk_cols = (n_heads + n_kv) * hd
    w_qkv = jnp.concatenate([_pair_lanes(attn_w_qkv[..., :qk_cols], hd), attn_w_qkv[..., qk_cols:]],
                            axis=-1).astype(BF16)
    q_gain = _pair_lanes(attn_q_norm, hd) * (sm_scale * LOG2_E)
    k_gain = _pair_lanes(attn_k_norm, hd)
    w_attn_out = attn_w_out.astype(BF16)
    w_conv_in, w_conv_out = conv_w_in.astype(BF16), conv_w_out.astype(BF16)
    d_ff = mlp_w_in.shape[-1]
    w_mlp_in = mlp_w_in.reshape(depth, d, d_ff // MLP_HIDDEN_CHUNK, MLP_HIDDEN_CHUNK)
    w_mlp_in = w_mlp_in.transpose(0, 2, 1, 3).astype(BF16)
    w_mlp_out = mlp_w_out.astype(BF16)

    for i in range(depth):
        j = i // n_mixers
        g, b = row(ln_g[i, 0]), row(ln_b[i, 0])
        if i % n_mixers == 0:
            x = _conv_layer(x, mods, 2 * i, w_conv_in, w_conv_out, j, row(conv_b_in[j]), conv_dw[j],
                            row(conv_dw_b[j]), row(conv_ln_g[j]), row(conv_ln_b[j]), g, b, alpha)
        else:
            q, k, v = _qkv_proj(x, mods, 2 * i, w_qkv, j, row(q_gain[j]), row(k_gain[j]), cos, sin,
                                n_heads, n_kv, hd)
            o = _attention(q, k, v, n_kv, hd)
            x = _proj_norm(o, x, mods, 2 * i, w_attn_out, j, g, b, alpha)
        x = _mlp_layer(x, mods, 2 * i + 1, w_mlp_in, w_mlp_out, i, row(ln_g[i, 1]), row(ln_b[i, 1]),
                       alpha)
    return x
```

```python
import functools

import jax
import jax.numpy as jnp
from jax import lax
from jax.experimental import pallas as pl
from jax.experimental.pallas import tpu as pltpu

F32 = jnp.float32
BF16 = jnp.bfloat16

GRID_W = 64
ROPE_THETA = 10000.0
LN_EPS = 1e-5
RMS_EPS = 1e-6
LOG2_E = 1.4426950408889634

V7X_VMEM_BYTES = 64 * 1024 * 1024
VMEM_LIMIT_BYTES = V7X_VMEM_BYTES - 8 * 1024 * 1024
LANES = 128
CONV_HALO_ROWS = 16
MLP_HIDDEN_CHUNK = 512


def _params(*semantics):
    return pltpu.CompilerParams(dimension_semantics=semantics, vmem_limit_bytes=VMEM_LIMIT_BYTES)


def _layer_norm(z, g, b):
    mu = jnp.mean(z, axis=-1, keepdims=True)
    zc = z - mu
    var = jnp.mean(zc * zc, axis=-1, keepdims=True)
    return zc * lax.rsqrt(var + LN_EPS) * g + b


def _modulate(x, m_ref):
    return x * (1.0 + m_ref[0, 1:2, :]) + m_ref[0, 0:1, :]


def _residual_norm(x, y, m_ref, g_ref, b_ref, alpha):
    z = alpha * x + (1.0 + m_ref[0, 2:3, :]) * y
    return _layer_norm(z, g_ref[...], b_ref[...])


def _mod_kernel(c_ref, w_ref, b_ref, o_ref):
    c = c_ref[...]
    c_act = (c * jax.nn.sigmoid(c)).astype(BF16)
    w = w_ref[0].astype(BF16)
    o_ref[0] = jnp.dot(c_act, w, preferred_element_type=F32) + b_ref[0]


def _modulation_all(c, mod_w, mod_b, tn=768):
    depth, two, d, d3 = mod_w.shape
    n_sub = depth * two
    bsz = c.shape[0]
    w = mod_w.reshape(n_sub, d, d3)
    b = mod_b.reshape(n_sub, 1, d3)
    out = pl.pallas_call(
        _mod_kernel,
        grid=(n_sub, d3 // tn),
        in_specs=[
            pl.BlockSpec((bsz, d), lambda l, n: (0, 0)),
            pl.BlockSpec((1, d, tn), lambda l, n: (l, 0, n)),
            pl.BlockSpec((1, 1, tn), lambda l, n: (l, 0, n)),
        ],
        out_specs=pl.BlockSpec((1, bsz, tn), lambda l, n: (l, 0, n)),
        out_shape=jax.ShapeDtypeStruct((n_sub, bsz, d3), F32),
        compiler_params=_params("arbitrary", "arbitrary"),
        name="modulation",
    )(c, w, b)
    return out.reshape(n_sub, bsz, 3, d)


def _mlp_kernel(x_ref, m_ref, win_ref, wout_ref, g_ref, b_ref, o_ref, h_ref, *, alpha, rb):
    f = pl.program_id(2)
    last = pl.num_programs(2) - 1
    tm = x_ref.shape[1]

    def hidden(h):
        u = jnp.maximum(jnp.dot(h, win_ref[...], preferred_element_type=F32), 0.0)
        return jnp.dot((u * u).astype(BF16), wout_ref[...], preferred_element_type=F32)

    @pl.when(f == 0)
    def _():
        for r0 in range(0, tm, rb):
            h = _modulate(x_ref[0, r0:r0 + rb, :], m_ref).astype(BF16)
            h_ref[r0:r0 + rb, :] = h
            o_ref[0, r0:r0 + rb, :] = hidden(h)

    @pl.when(jnp.logical_and(f > 0, f < last))
    def _():
        o_ref[0] += hidden(h_ref[...])

    @pl.when(f == last)
    def _():
        for r0 in range(0, tm, rb):
            y = o_ref[0, r0:r0 + rb, :] + hidden(h_ref[r0:r0 + rb, :])
            o_ref[0, r0:r0 + rb, :] = _residual_norm(x_ref[0, r0:r0 + rb, :], y, m_ref, g_ref,
                                                     b_ref, alpha)


def _mlp_layer(x, mods, sub, w_in, w_out, layer, g, b, alpha, tm=1024, tf=MLP_HIDDEN_CHUNK, rb=256):
    bsz, seq, d = x.shape
    n_f = w_in.shape[-1] // tf
    assert n_f >= 2 and tm % rb == 0
    return pl.pallas_call(
        functools.partial(_mlp_kernel, alpha=alpha, rb=rb),
        grid=(bsz, seq // tm, n_f),
        in_specs=[
            pl.BlockSpec((1, tm, d), lambda bi, i, f: (bi, i, 0)),
            pl.BlockSpec((None, 1, 3, d), lambda bi, i, f: (sub, bi, 0, 0)),
            pl.BlockSpec((None, d, tf), lambda bi, i, f: (layer, 0, f)),
            pl.BlockSpec((None, tf, d), lambda bi, i, f: (layer, f, 0)),
            pl.BlockSpec((1, d), lambda bi, i, f: (0, 0)),
            pl.BlockSpec((1, d), lambda bi, i, f: (0, 0)),
        ],
        out_specs=pl.BlockSpec((1, tm, d), lambda bi, i, f: (bi, i, 0)),
        out_shape=jax.ShapeDtypeStruct(x.shape, F32),
        scratch_shapes=[pltpu.VMEM((tm, d), BF16)],
        compiler_params=_params("parallel", "parallel", "arbitrary"),
        name="mlp",
    )(x, mods, w_in, w_out, g, b)


def _proj_norm_kernel(a_ref, x_ref, m_ref, w_ref, g_ref, b_ref, o_ref, *, alpha, rb):
    for r0 in range(0, x_ref.shape[1], rb):
        y = jnp.dot(a_ref[0, r0:r0 + rb, :], w_ref[...], preferred_element_type=F32)
        o_ref[0, r0:r0 + rb, :] = _residual_norm(x_ref[0, r0:r0 + rb, :], y, m_ref, g_ref, b_ref,
                                                 alpha)


def _proj_norm(a, x, mods, sub, w, layer, g, b, alpha, tm=512, rb=256):
    bsz, seq, d = x.shape
    k = a.shape[-1]
    assert tm % rb == 0
    return pl.pallas_call(
        functools.partial(_proj_norm_kernel, alpha=alpha, rb=rb),
        grid=(bsz, seq // tm),
        in_specs=[
            pl.BlockSpec((1, tm, k), lambda bi, i: (bi, i, 0)),
            pl.BlockSpec((1, tm, d), lambda bi, i: (bi, i, 0)),
            pl.BlockSpec((None, 1, 3, d), lambda bi, i: (sub, bi, 0, 0)),
            pl.BlockSpec((None, k, d), lambda bi, i: (layer, 0, 0)),
            pl.BlockSpec((1, d), lambda bi, i: (0, 0)),
            pl.BlockSpec((1, d), lambda bi, i: (0, 0)),
        ],
        out_specs=pl.BlockSpec((1, tm, d), lambda bi, i: (bi, i, 0)),
        out_shape=jax.ShapeDtypeStruct(x.shape, F32),
        compiler_params=_params("parallel", "parallel"),
        name="proj_norm",
    )(a, x, mods, w, g, b)


def _qkv_kernel(x_ref, m_ref, w_ref, qg_ref, kg_ref, cos_ref, sin_ref, q_ref, k_ref, v_ref,
                *, n_heads, n_kv, hd, rb):
    k0 = n_heads * hd
    v0 = k0 + n_kv * hd
    for r0 in range(0, x_ref.shape[1], rb):
        rws = slice(r0, r0 + rb)
        h = _modulate(x_ref[0, rws, :], m_ref).astype(BF16)
        qkv = jnp.dot(h, w_ref[...], preferred_element_type=F32)
        cos = cos_ref[rws, :]
        sin = sin_ref[rws, :]

        def norm_rope(xh, gain):
            ms = jnp.mean(xh * xh, axis=-1, keepdims=True)
            y = xh * lax.rsqrt(ms + RMS_EPS) * gain
            return y * cos + pltpu.roll(y, hd // 2, 1) * sin

        for hi in range(n_heads):
            q_ref[0, rws, hi * hd:(hi + 1) * hd] = norm_rope(
                qkv[:, hi * hd:(hi + 1) * hd], qg_ref[...]).astype(BF16)
        for hi in range(n_kv):
            k_ref[0, rws, hi * hd:(hi + 1) * hd] = norm_rope(
                qkv[:, k0 + hi * hd:k0 + (hi + 1) * hd], kg_ref[...]).astype(BF16)
        v_ref[0, rws, :] = qkv[:, v0:].astype(BF16)


def _qkv_proj(x, mods, sub, w_qkv, layer, q_gain, k_gain, cos, sin, n_heads, n_kv, hd, tm=512,
              rb=256):
    bsz, seq, d = x.shape
    qkv_dim = w_qkv.shape[-1]
    assert tm % rb == 0
    return pl.pallas_call(
        functools.partial(_qkv_kernel, n_heads=n_heads, n_kv=n_kv, hd=hd, rb=rb),
        grid=(bsz, seq // tm),
        in_specs=[
            pl.BlockSpec((1, tm, d), lambda bi, i: (bi, i, 0)),
            pl.BlockSpec((None, 1, 3, d), lambda bi, i: (sub, bi, 0, 0)),
            pl.BlockSpec((None, d, qkv_dim), lambda bi, i: (layer, 0, 0)),
            pl.BlockSpec((1, hd), lambda bi, i: (0, 0)),
            pl.BlockSpec((1, hd), lambda bi, i: (0, 0)),
            pl.BlockSpec((tm, hd), lambda bi, i: (i, 0)),
            pl.BlockSpec((tm, hd), lambda bi, i: (i, 0)),
        ],
        out_specs=[
            pl.BlockSpec((1, tm, n_heads * hd), lambda bi, i: (bi, i, 0)),
            pl.BlockSpec((1, tm, n_kv * hd), lambda bi, i: (bi, i, 0)),
            pl.BlockSpec((1, tm, n_kv * hd), lambda bi, i: (bi, i, 0)),
        ],
        out_shape=[
            jax.ShapeDtypeStruct((bsz, seq, n_heads * hd), BF16),
            jax.ShapeDtypeStruct((bsz, seq, n_kv * hd), BF16),
            jax.ShapeDtypeStruct((bsz, seq, n_kv * hd), BF16),
        ],
        compiler_params=_params("parallel", "parallel"),
        name="qkv_proj",
    )(x, mods, w_qkv, q_gain, k_gain, cos, sin)


def _attn_kernel(q_ref, k_ref, v_ref, o_ref, s_ref, vt_ref, *, group, hd, tq, unroll):
    seq = k_ref.shape[1]
    n_tiles = seq // tq
    half = seq // 2
    vt_ref[...] = v_ref[0].T

    def logits_t(buf, row0, gi):
        q = q_ref[0, pl.ds(row0, tq), gi * hd:(gi + 1) * hd]
        for h0 in (0, half):
            s_ref[buf, h0:h0 + half, :] = lax.dot_general(
                k_ref[0, h0:h0 + half, :], q, (((1,), (1,)), ((), ())),
                preferred_element_type=F32)

    logits_t(0, 0, 0)

    def tiles(it, carry):
        for sub in range(unroll):
            row0 = pl.multiple_of((it * unroll + sub) * tq, tq)
            row_next = pl.multiple_of(jnp.minimum(it * unroll + sub + 1, n_tiles - 1) * tq, tq)
            for gi in range(group):
                cur = gi % 2
                if gi + 1 < group:
                    logits_t(1 - cur, row0, gi + 1)
                else:
                    logits_t(1 - cur, row_next, 0)
                s = s_ref[cur]
                p = jnp.exp2(s - jnp.max(s, axis=0, keepdims=True))
                denom = jnp.sum(p, axis=0, keepdims=True)
                ot = jnp.dot(vt_ref[...], p.astype(BF16), preferred_element_type=F32)
                o_ref[0, pl.ds(row0, tq), gi * hd:(gi + 1) * hd] = (ot / denom).T.astype(BF16)
        return carry

    lax.fori_loop(0, n_tiles // unroll, tiles, 0)


def _attention(q, k, v, n_kv, hd, tq=256, unroll=4):
    bsz, seq, qd = q.shape
    group = qd // (n_kv * hd)
    assert group % 2 == 0, "logit buffers alternate per head and must realign every query tile"
    assert (seq // tq) % unroll == 0
    gw = group * hd
    return pl.pallas_call(
        functools.partial(_attn_kernel, group=group, hd=hd, tq=tq, unroll=unroll),
        grid=(bsz, n_kv),
        in_specs=[
            pl.BlockSpec((1, seq, gw), lambda bi, kv: (bi, 0, kv)),
            pl.BlockSpec((1, seq, hd), lambda bi, kv: (bi, 0, kv)),
            pl.BlockSpec((1, seq, hd), lambda bi, kv: (bi, 0, kv)),
        ],
        out_specs=pl.BlockSpec((1, seq, gw), lambda bi, kv: (bi, 0, kv)),
        out_shape=jax.ShapeDtypeStruct(q.shape, BF16),
        scratch_shapes=[pltpu.VMEM((2, seq, tq), F32), pltpu.VMEM((hd, seq), BF16)],
        compiler_params=_params("parallel", "parallel"),
        name="attention",
    )(q, k, v)


def _conv_layer_kernel(x_ref, xp_ref, xn_ref, m_ref, win_ref, bin_ref, dw_ref, dwb_ref, cg_ref,
                       cb_ref, wout_ref, g_ref, b_ref, o_ref, h_ref, buf_ref, c_ref,
                       *, alpha, width, rows, chunk):
    i = pl.program_id(1)
    ts, d = x_ref.shape[1:]
    halo = CONV_HALO_ROWS
    pad = width // 2
    first = i == 0
    last = i == pl.num_programs(1) - 1

    h_ref[0:halo, :] = _modulate(xp_ref[0], m_ref).astype(BF16)
    h_ref[halo:halo + ts, :] = _modulate(x_ref[0], m_ref).astype(BF16)
    h_ref[halo + ts:, :] = _modulate(xn_ref[0], m_ref).astype(BF16)

    for c0 in range(0, d, chunk):
        h = h_ref[...]
        a = jnp.dot(h, win_ref[:, c0:c0 + chunk], preferred_element_type=F32) + bin_ref[:, c0:c0 + chunk]
        gate = jnp.dot(h, win_ref[:, d + c0:d + c0 + chunk],
                       preferred_element_type=F32) + bin_ref[:, d + c0:d + c0 + chunk]
        glu = a * jax.nn.sigmoid(gate)
        for l0 in range(0, chunk, LANES):
            j = (c0 + l0) // LANES
            buf_ref[j, 0:halo, :] = jnp.where(first, 0.0, glu[0:halo, l0:l0 + LANES])
            buf_ref[j, halo:halo + ts, :] = glu[halo:halo + ts, l0:l0 + LANES]
            buf_ref[j, halo + ts:, :] = jnp.where(last, 0.0, glu[halo + ts:, l0:l0 + LANES])
        for l0 in range(0, chunk, LANES):
            j = (c0 + l0) // LANES
            for r0 in range(0, ts, rows):
                acc = jnp.zeros((rows, LANES), F32)
                for t in range(width):
                    tap = buf_ref[j, r0 + halo - pad + t:r0 + halo - pad + t + rows, :]
                    acc = acc + tap * dw_ref[j, t:t + 1, :]
                c_ref[r0:r0 + rows, c0 + l0:c0 + l0 + LANES] = acc

    v = _layer_norm(c_ref[...] + dwb_ref[...], cg_ref[...], cb_ref[...])
    v = (v * jax.nn.sigmoid(v)).astype(BF16)
    y = jnp.dot(v, wout_ref[...], preferred_element_type=F32)
    o_ref[0] = _residual_norm(x_ref[0], y, m_ref, g_ref, b_ref, alpha)


def _conv_layer(x, mods, sub, w_in, w_out, layer, b_in, dw, dw_b, cg, cb, g, b, alpha,
                ts=512, rows=64, chunk=512):
    bsz, seq, d = x.shape
    width = dw.shape[0]
    halo = CONV_HALO_ROWS
    assert width // 2 <= halo and ts % halo == 0 and d % chunk == 0 and chunk % LANES == 0
    nl = d // LANES
    per = ts // halo
    n_halo = seq // halo
    dw3 = dw.reshape(width, nl, LANES).transpose(1, 0, 2)
    const = lambda bi, i: (0, 0)
    resident = pl.Buffered(1)
    return pl.pallas_call(
        functools.partial(_conv_layer_kernel, alpha=alpha, width=width, rows=rows, chunk=chunk),
        grid=(bsz, seq // ts),
        in_specs=[
            pl.BlockSpec((1, ts, d), lambda bi, i: (bi, i, 0)),
            pl.BlockSpec((1, halo, d), lambda bi, i: (bi, jnp.maximum(i * per - 1, 0), 0)),
            pl.BlockSpec((1, halo, d), lambda bi, i: (bi, jnp.minimum((i + 1) * per, n_halo - 1), 0)),
            pl.BlockSpec((None, 1, 3, d), lambda bi, i: (sub, bi, 0, 0)),
            pl.BlockSpec((None, d, 2 * d), lambda bi, i: (layer, 0, 0), pipeline_mode=resident),
            pl.BlockSpec((1, 2 * d), const),
            pl.BlockSpec((nl, width, LANES), lambda bi, i: (0, 0, 0)),
            pl.BlockSpec((1, d), const),
            pl.BlockSpec((1, d), const),
            pl.BlockSpec((1, d), const),
            pl.BlockSpec((None, d, d), lambda bi, i: (layer, 0, 0), pipeline_mode=resident),
            pl.BlockSpec((1, d), const),
            pl.BlockSpec((1, d), const),
        ],
        out_specs=pl.BlockSpec((1, ts, d), lambda bi, i: (bi, i, 0)),
        out_shape=jax.ShapeDtypeStruct(x.shape, F32),
        scratch_shapes=[
            pltpu.VMEM((ts + 2 * halo, d), BF16),
            pltpu.VMEM((nl, ts + 2 * halo, LANES), F32),
            pltpu.VMEM((ts, d), F32),
        ],
        compiler_params=_params("parallel", "parallel"),
        name="conv_layer",
    )(x, x, x, mods, w_in, b_in, dw3, dw_b, cg, cb, w_out, g, b)


def _rope_tables(seq, hd):
    axis_dim = hd // 2
    half = axis_dim // 2
    t = jnp.arange(seq, dtype=jnp.int32)
    inv_freq = ROPE_THETA ** (-jnp.arange(half, dtype=F32) / half)
    ang_r = (t // GRID_W).astype(F32)[:, None] * inv_freq[None, :]
    ang_c = (t % GRID_W).astype(F32)[:, None] * inv_freq[None, :]
    cos_r, sin_r, cos_c, sin_c = jnp.cos(ang_r), jnp.sin(ang_r), jnp.cos(ang_c), jnp.sin(ang_c)
    cos = jnp.concatenate([cos_r, cos_c, cos_r, cos_c], axis=-1)
    sin = jnp.concatenate([-sin_r, -sin_c, sin_r, sin_c], axis=-1)
    return cos, sin


def _pair_lanes(a, hd):
    lead = a.shape[:-1]
    quarter = hd // 4
    a = a.reshape(*lead, -1, 2, 2, quarter)
    return jnp.swapaxes(a, -3, -2).reshape(*lead, -1)


def kernel(x, c, mod_w, mod_b, ln_g, ln_b, conv_w_in, conv_b_in, conv_dw, conv_dw_b, conv_ln_g,
           conv_ln_b, conv_w_out, attn_w_qkv, attn_q_norm, attn_k_norm, attn_w_out, mlp_w_in,
           mlp_w_out):
    depth = mod_w.shape[0]
    n_mixers = 2
    bsz, seq, d = x.shape
    hd = attn_q_norm.shape[-1]
    n_heads = attn_w_out.shape[1] // hd
    n_kv = (attn_w_qkv.shape[-1] - n_heads * hd) // (2 * hd)
    alpha = (2 * depth) ** 0.25
    sm_scale = hd ** -0.5

    mods = _modulation_all(c, mod_w, mod_b)
    cos, sin = _rope_tables(seq, hd)
    row = lambda a: a.reshape(1, -1)

    qk_cols = (n_heads + n_kv) * hd
    w_qkv = jnp.concatenate([_pair_lanes(attn_w_qkv[..., :qk_cols], hd), attn_w_qkv[..., qk_cols:]],
                            axis=-1).astype(BF16)
    q_gain = _pair_lanes(attn_q_norm, hd) * (sm_scale * LOG2_E)
    k_gain = _pair_lanes(attn_k_norm, hd)
    w_attn_out = attn_w_out.astype(BF16)
    w_conv_in, w_conv_out = conv_w_in.astype(BF16), conv_w_out.astype(BF16)
    w_mlp_in, w_mlp_out = mlp_w_in.astype(BF16), mlp_w_out.astype(BF16)

    for i in range(depth):
        j = i // n_mixers
        g, b = row(ln_g[i, 0]), row(ln_b[i, 0])
        if i % n_mixers == 0:
            x = _conv_layer(x, mods, 2 * i, w_conv_in, w_conv_out, j, row(conv_b_in[j]), conv_dw[j],
                            row(conv_dw_b[j]), row(conv_ln_g[j]), row(conv_ln_b[j]), g, b, alpha)
        else:
            q, k, v = _qkv_proj(x, mods, 2 * i, w_qkv, j, row(q_gain[j]), row(k_gain[j]), cos, sin,
                                n_heads, n_kv, hd)
            o = _attention(q, k, v, n_kv, hd)
            x = _proj_norm(o, x, mods, 2 * i, w_attn_out, j, g, b, alpha)
        x = _mlp_layer(x, mods, 2 * i + 1, w_mlp_in, w_mlp_out, i, row(ln_g[i, 1]), row(ln_b[i, 1]),
                       alpha)
    return x
```
